```python
import math
import jax, jax.numpy as jnp
from jax import lax
import numpy as np

D_MODEL = 1024
BATCH = 4
SEQ = 4096
DEPTH = 4
DEC_BATCH = 1
DEC_SEQ = 16384
PAST_LEN = 128

PLE_DIM = 256
N_MIXERS = 2
N_SSD_LAYERS = (DEPTH + 1) // 2
N_ATT_LAYERS = DEPTH // 2
EPS = 1e-6

SSD_EXPAND = 2
SSD_INNER = SSD_EXPAND * D_MODEL
SSD_HEAD_DIM = 64
SSD_HEADS = SSD_INNER // SSD_HEAD_DIM
SSD_GROUPS = 8
SSD_HPG = SSD_HEADS // SSD_GROUPS
SSD_STATE = 128
SSD_GN = SSD_GROUPS * SSD_STATE
SSD_CONV_W = 5
SSD_CHUNK = 128
SSD_CONV_CH = SSD_INNER + 2 * SSD_GN
SSD_IN_COLS = SSD_INNER + SSD_CONV_CH + 2 * SSD_HEADS

ATT_WIDTH = D_MODEL
ATT_HEADS = 8
ATT_QK_DIM = ATT_WIDTH // (2 * ATT_HEADS)
ATT_V_DIM = 2 * ATT_QK_DIM
ATT_HD = ATT_HEADS * ATT_QK_DIM
ATT_Q_BLOCK = 128
ATT_IN_COLS = 4 * ATT_HD + ATT_HEADS * ATT_V_DIM + ATT_WIDTH

kernel_name = "hybrid_ssd_diffattn_bidir_encoder"


def rms_norm(x, g):
    xf = x.astype(jnp.float32)
    y = xf * lax.rsqrt(jnp.mean(xf * xf, axis=-1, keepdims=True) + EPS)
    return (y * g.astype(jnp.float32)).astype(x.dtype)


def centred_dwconv(x, w, b):
    pad = SSD_CONV_W // 2
    y = lax.conv_general_dilated(x, w[:, None, :].astype(x.dtype), window_strides=(1,),
                                 padding=[(pad, pad)],
                                 dimension_numbers=('NWC', 'WIO', 'NWC'),
                                 feature_group_count=x.shape[-1])
    return y + b.astype(x.dtype)


def ssd_chunked(x, dt, a, bm, cm):
    b, l = x.shape[0], x.shape[1]
    q = SSD_CHUNK
    nc = l // q
    xc = x.reshape(b, nc, q, SSD_GROUPS, SSD_HPG, SSD_HEAD_DIM)
    dtc = dt.reshape(b, nc, q, SSD_GROUPS, SSD_HPG)
    bc = bm.reshape(b, nc, q, SSD_GROUPS, SSD_STATE)
    cc = cm.reshape(b, nc, q, SSD_GROUPS, SSD_STATE)
    a_cs = jnp.cumsum(jnp.moveaxis(dtc * a, 2, -1), axis=-1)
    xdt = xc.astype(jnp.float32) * dtc[..., None]
    seg = a_cs[..., :, None] - a_cs[..., None, :]
    lower = jnp.tril(jnp.ones((q, q), dtype=bool))
    lmat = jnp.exp(jnp.where(lower, seg, -jnp.inf))
    cb = jnp.einsum('bclgn,bcsgn->bcgls', cc, bc).astype(jnp.float32)
    y_diag = jnp.einsum('bcgls,bcgrls,bcsgrp->bclgrp', cb, lmat, xdt)
    decay_states = jnp.exp(a_cs[..., -1:] - a_cs)
    states = jnp.einsum('bcsgn,bcgrs,bcsgrp->bcgrpn', bc.astype(jnp.float32), decay_states, xdt)
    chunk_decay = jnp.exp(a_cs[..., -1])

    def step(carry, inp):
        st, dec = inp
        return carry * dec[..., None, None] + st, carry

    init = jnp.zeros_like(states[:, 0])
    _, prev = lax.scan(step, init, (jnp.moveaxis(states, 1, 0), jnp.moveaxis(chunk_decay, 1, 0)))
    prev = jnp.moveaxis(prev, 0, 1)
    y_off = jnp.einsum('bclgn,bcgrpn,bcgrl->bclgrp', cc.astype(jnp.float32), prev, jnp.exp(a_cs))
    return (y_diag + y_off).reshape(b, l, SSD_GROUPS, SSD_HPG, SSD_HEAD_DIM)


def ssd_mixer(h, w_in, conv_w, conv_b, dt_bias, a_log, d_skip, norm_g, w_out):
    b, l, _ = h.shape
    proj = h @ w_in
    z = proj[..., :SSD_INNER]
    xbc = proj[..., SSD_INNER:SSD_INNER + SSD_CONV_CH]
    dt_raw = proj[..., SSD_INNER + SSD_CONV_CH:]
    xbc = jax.nn.silu(centred_dwconv(xbc, conv_w, conv_b))
    xs = xbc[..., :SSD_INNER].reshape(b, l, SSD_GROUPS, SSD_HPG, SSD_HEAD_DIM)
    bm = xbc[..., SSD_INNER:SSD_INNER + SSD_GN].reshape(b, l, SSD_GROUPS, SSD_STATE)
    cm = xbc[..., SSD_INNER + SSD_GN:].reshape(b, l, SSD_GROUPS, SSD_STATE)
    dt = jax.nn.softplus(dt_raw.astype(jnp.float32).reshape(b, l, 2, SSD_GROUPS, SSD_HPG)
                         + dt_bias.astype(jnp.float32).reshape(2, SSD_GROUPS, SSD_HPG))
    a = -jnp.exp(a_log.astype(jnp.float32)).reshape(2, SSD_GROUPS, SSD_HPG)
    y_f = ssd_chunked(xs, dt[:, :, 0], a[0], bm, cm)
    fl = lambda t: jnp.flip(t, axis=1)
    y_b = fl(ssd_chunked(fl(xs), fl(dt[:, :, 1]), a[1], fl(bm), fl(cm)))
    y = y_f + y_b + xs.astype(jnp.float32) * d_skip.astype(jnp.float32).reshape(SSD_GROUPS, SSD_HPG)[..., None]
    y = y.reshape(b, l, SSD_INNER).astype(h.dtype)
    y = rms_norm(y * jax.nn.silu(z), norm_g)
    return y @ w_out


def alibi_slopes():
    return jnp.asarray(np.array([2.0 ** (-8.0 * (i + 1) / ATT_HEADS) for i in range(ATT_HEADS)],
                                dtype=np.float32))


def diff_attention_mixer(h, w_in, q_norm_g, k_norm_g, lam_q1, lam_k1, lam_q2, lam_k2,
                         sub_norm_g, w_out, lambda_init):
    b, l, _ = h.shape
    proj = h @ w_in
    q = proj[..., :2 * ATT_HD].reshape(b, l, ATT_HEADS, 2, ATT_QK_DIM)
    k = proj[..., 2 * ATT_HD:4 * ATT_HD].reshape(b, l, ATT_HEADS, 2, ATT_QK_DIM)
    v = proj[..., 4 * ATT_HD:4 * ATT_HD + ATT_HEADS * ATT_V_DIM].reshape(b, l, ATT_HEADS, ATT_V_DIM)
    gate = proj[..., 4 * ATT_HD + ATT_HEADS * ATT_V_DIM:]
    q = rms_norm(q, q_norm_g) * (ATT_QK_DIM ** -0.5)
    k = rms_norm(k, k_norm_g)
    f32 = jnp.float32
    lam = (jnp.exp(jnp.sum(lam_q1.astype(f32) * lam_k1.astype(f32)))
           - jnp.exp(jnp.sum(lam_q2.astype(f32) * lam_k2.astype(f32))) + lambda_init)
    slopes = alibi_slopes()
    nq = l // ATT_Q_BLOCK
    qb = jnp.moveaxis(q.reshape(b, nq, ATT_Q_BLOCK, ATT_HEADS, 2, ATT_QK_DIM), 1, 0)
    pos_k = jnp.arange(l)

    def block(args):
        qi, start = args
        s = jnp.einsum('bqhmd,bkhmd->bhmqk', qi, k).astype(f32)
        pos_q = start + jnp.arange(ATT_Q_BLOCK)
        dist = jnp.abs(pos_q[:, None] - pos_k[None, :]).astype(f32)
        s = s - slopes[None, :, None, None, None] * dist
        pr = jax.nn.softmax(s, axis=-1)
        attn = pr[:, :, 0] - lam * pr[:, :, 1]
        return jnp.einsum('bhqk,bkhe->bqhe', attn.astype(v.dtype), v)

    starts = jnp.arange(nq, dtype=jnp.int32) * ATT_Q_BLOCK
    o = lax.map(block, (qb, starts))
    o = jnp.moveaxis(o, 0, 1).reshape(b, l, ATT_HEADS, ATT_V_DIM)
    o = rms_norm(o, sub_norm_g) * (1.0 - lambda_init)
    o = o.reshape(b, l, ATT_WIDTH) * jax.nn.silu(gate)
    return o @ w_out


def trunk(x, p, pre_norm_g, ssd_w_in, ssd_conv_w, ssd_conv_b, ssd_dt_bias, ssd_a_log,
          ssd_d_skip, ssd_norm_g, ssd_w_out, att_w_in, att_q_norm_g, att_k_norm_g,
          att_lam_q1, att_lam_k1, att_lam_q2, att_lam_k2, att_sub_norm_g, att_w_out,
          ple_w_proj, ple_norm_g, ple_gate_norm_g, ple_w_gate):
    for i in range(DEPTH):
        h = rms_norm(x, pre_norm_g[i])
        j = i // N_MIXERS
        if i % N_MIXERS == 0:
            mix = ssd_mixer(h, ssd_w_in[j], ssd_conv_w[j], ssd_conv_b[j], ssd_dt_bias[j],
                            ssd_a_log[j], ssd_d_skip[j], ssd_norm_g[j], ssd_w_out[j])
        else:
            lambda_init = 0.8 - 0.6 * math.exp(-0.3 * i)
            mix = diff_attention_mixer(h, att_w_in[j], att_q_norm_g[j], att_k_norm_g[j],
                                       att_lam_q1[j], att_lam_k1[j], att_lam_q2[j], att_lam_k2[j],
                                       att_sub_norm_g[j], att_w_out[j], lambda_init)
        x = x + mix.astype(x.dtype)
        g = jax.nn.sigmoid(rms_norm(x, ple_gate_norm_g[i]) @ ple_w_gate[i])
        e = rms_norm(p[i] @ ple_w_proj[i], ple_norm_g[i])
        x = x + g * e
    return x


def setup_inputs(seed: int = 0) -> dict:
    key = jax.random.key(seed)
    ks = iter(jax.random.split(key, 40))
    nrm = lambda shape, scale: jax.random.normal(next(ks), shape, jnp.float32) * scale
    gain = lambda shape: 1.0 + nrm(shape, 0.05)
    NA, NB = N_SSD_LAYERS, N_ATT_LAYERS
    dt0 = jnp.exp(jax.random.uniform(next(ks), (NA, 2, SSD_HEADS), jnp.float32,
                                     math.log(1e-3), math.log(1e-1)))
    dt_bias = dt0 + jnp.log(-jnp.expm1(-dt0))
    a_log = jnp.log(jax.random.uniform(next(ks), (NA, 2, SSD_HEADS), jnp.float32, 1.0, 16.0))
    return {
        "x_prompt": nrm((BATCH, SEQ, D_MODEL), 1.0),
        "x_sample": nrm((DEC_BATCH, DEC_SEQ, D_MODEL), 1.0),
        "p_prompt": nrm((DEPTH, BATCH, SEQ, PLE_DIM), 1.0),
        "p_sample": nrm((DEPTH, DEC_BATCH, DEC_SEQ, PLE_DIM), 1.0),
        "pre_norm_g": gain((DEPTH, D_MODEL)),
        "ssd_w_in": nrm((NA, D_MODEL, SSD_IN_COLS), D_MODEL ** -0.5),
        "ssd_conv_w": nrm((NA, SSD_CONV_W, SSD_CONV_CH), SSD_CONV_W ** -0.5),
        "ssd_conv_b": nrm((NA, SSD_CONV_CH), 0.02),
        "ssd_dt_bias": dt_bias,
        "ssd_a_log": a_log,
        "ssd_d_skip": gain((NA, SSD_HEADS)),
        "ssd_norm_g": gain((NA, SSD_INNER)),
        "ssd_w_out": nrm((NA, SSD_INNER, D_MODEL), SSD_INNER ** -0.5),
        "att_w_in": nrm((NB, D_MODEL, ATT_IN_COLS), D_MODEL ** -0.5),
        "att_q_norm_g": gain((NB, ATT_QK_DIM)),
        "att_k_norm_g": gain((NB, ATT_QK_DIM)),
        "att_lam_q1": nrm((NB, ATT_QK_DIM), 0.1),
        "att_lam_k1": nrm((NB, ATT_QK_DIM), 0.1),
        "att_lam_q2": nrm((NB, ATT_QK_DIM), 0.1),
        "att_lam_k2": nrm((NB, ATT_QK_DIM), 0.1),
        "att_sub_norm_g": gain((NB, ATT_V_DIM)),
        "att_w_out": nrm((NB, ATT_WIDTH, D_MODEL), ATT_WIDTH ** -0.5),
        "ple_w_proj": nrm((DEPTH, PLE_DIM, D_MODEL), PLE_DIM ** -0.5),
        "ple_norm_g": gain((DEPTH, D_MODEL)),
        "ple_gate_norm_g": gain((DEPTH, D_MODEL)),
        "ple_w_gate": nrm((DEPTH, D_MODEL, D_MODEL), D_MODEL ** -0.5),
    }


def reference(x_prompt, x_sample, p_prompt, p_sample, pre_norm_g, ssd_w_in, ssd_conv_w,
              ssd_conv_b, ssd_dt_bias, ssd_a_log, ssd_d_skip, ssd_norm_g, ssd_w_out,
              att_w_in, att_q_norm_g, att_k_norm_g, att_lam_q1, att_lam_k1, att_lam_q2,
              att_lam_k2, att_sub_norm_g, att_w_out, ple_w_proj, ple_norm_g,
              ple_gate_norm_g, ple_w_gate):
    y_prompt = trunk(x_prompt, p_prompt, pre_norm_g, ssd_w_in, ssd_conv_w, ssd_conv_b,
                     ssd_dt_bias, ssd_a_log, ssd_d_skip, ssd_norm_g, ssd_w_out, att_w_in,
                     att_q_norm_g, att_k_norm_g, att_lam_q1, att_lam_k1, att_lam_q2,
                     att_lam_k2, att_sub_norm_g, att_w_out, ple_w_proj, ple_norm_g,
                     ple_gate_norm_g, ple_w_gate)
    y_sample = trunk(x_sample, p_sample, pre_norm_g, ssd_w_in, ssd_conv_w, ssd_conv_b,
                     ssd_dt_bias, ssd_a_log, ssd_d_skip, ssd_norm_g, ssd_w_out, att_w_in,
                     att_q_norm_g, att_k_norm_g, att_lam_q1, att_lam_k1, att_lam_q2,
                     att_lam_k2, att_sub_norm_g, att_w_out, ple_w_proj, ple_norm_g,
                     ple_gate_norm_g, ple_w_gate)
    return (y_prompt, y_sample)
```

```python
import functools
import math

import numpy as np
import jax
import jax.numpy as jnp
from jax import lax
from jax.experimental import pallas as pl
from jax.experimental.pallas import tpu as pltpu

F32 = jnp.float32
BF16 = jnp.bfloat16

EPS = 1e-6
D_MODEL = 1024
DEPTH = 4
PLE_DIM = 256

SSD_INNER = 2048
SSD_HEAD_DIM = 64
SSD_HEADS = 32
SSD_GROUPS = 8
SSD_HPG = 4
SSD_STATE = 128
SSD_GN = SSD_GROUPS * SSD_STATE
SSD_CONV_W = 5
SSD_CHUNK = 128
SSD_MAIN_COLS = SSD_INNER + SSD_INNER + 2 * SSD_GN

ATT_HEADS = 8
ATT_QK_DIM = 64
ATT_V_DIM = 128
ATT_IN_COLS = 4096

LANES = 128
SUBLANES = 8
VMEM_LIMIT_BYTES = 56 * 1024 * 1024


def _cparams(*sem):
    return pltpu.CompilerParams(dimension_semantics=sem, vmem_limit_bytes=VMEM_LIMIT_BYTES)


def _tile(n, pref):
    t = min(n, pref)
    assert n % t == 0, (n, pref)
    return t


def _rms(x, g):
    return x * lax.rsqrt(jnp.mean(x * x, axis=-1, keepdims=True) + EPS) * g


def _silu(x):
    return x * jax.nn.sigmoid(x)


def _inproj_kernel(x_ref, g_ref, w_ref, *rest, has_dt):
    if has_dt:
        wdt_ref, dtb_ref, out_ref, dt_ref, hn_ref = rest
    else:
        out_ref, hn_ref = rest

    @pl.when(pl.program_id(1) == 0)
    def _():
        hn_ref[...] = _rms(x_ref[...], g_ref[...]).astype(BF16)
        if has_dt:
            raw = jnp.dot(hn_ref[...], wdt_ref[...], preferred_element_type=F32) + dtb_ref[...]
            dt_ref[...] = jax.nn.softplus(raw)

    out_ref[...] = jnp.dot(hn_ref[...], w_ref[...], preferred_element_type=F32)


def _inproj(x, g, w, wdt=None, dtb=None):
    t, d = x.shape
    n = w.shape[1]
    tm = _tile(t, 1024)
    tn = _tile(n, 2048)
    has_dt = wdt is not None
    in_specs = [
        pl.BlockSpec((tm, d), lambda i, j: (i, 0)),
        pl.BlockSpec((1, d), lambda i, j: (0, 0)),
        pl.BlockSpec((d, tn), lambda i, j: (0, j)),
    ]
    args = [x, g, w]
    out_shape = [jax.ShapeDtypeStruct((t, n), F32)]
    out_specs = [pl.BlockSpec((tm, tn), lambda i, j: (i, j))]
    if has_dt:
        in_specs += [pl.BlockSpec((d, LANES), lambda i, j: (0, 0)),
                     pl.BlockSpec((1, LANES), lambda i, j: (0, 0))]
        args += [wdt, dtb]
        out_shape.append(jax.ShapeDtypeStruct((t, LANES), F32))
        out_specs.append(pl.BlockSpec((tm, LANES), lambda i, j: (i, 0)))
    res = pl.pallas_call(
        functools.partial(_inproj_kernel, has_dt=has_dt),
        grid=(t // tm, n // tn),
        in_specs=in_specs,
        out_specs=out_specs,
        out_shape=out_shape,
        scratch_shapes=[pltpu.VMEM((tm, d), BF16)],
        compiler_params=_cparams("parallel", "arbitrary"),
        name="inproj_dt" if has_dt else "inproj",
    )(*args)
    return res if has_dt else res[0]


def _conv_kernel(prev_ref, cur_ref, next_ref, w_ref, b_ref, out_ref, *, nl):
    i = pl.program_id(1)
    tl = cur_ref.shape[0]
    prev = jnp.where(i > 0, prev_ref[...], 0.0)
    nxt = jnp.where(i < nl - 1, next_ref[...], 0.0)
    ext = jnp.concatenate([prev, cur_ref[...], nxt], axis=0)
    n = tl + 2 * SUBLANES
    acc = jnp.zeros(cur_ref.shape, F32) + b_ref[...]
    half = SSD_CONV_W // 2
    for k in range(SSD_CONV_W):
        shift = (half - k) % n
        r = ext if shift == 0 else pltpu.roll(ext, shift, 0)
        acc = acc + r[SUBLANES:SUBLANES + tl] * w_ref[k:k + 1, :]
    out_ref[...] = _silu(acc).astype(out_ref.dtype)


def _conv_silu(proj, w, b, nb, seq, col0_in, col0_w, ncols, out_dtype):
    t = proj.shape[0]
    cb = 512
    tl = _tile(seq, 512)
    nl = seq // tl
    rb = tl // SUBLANES
    nrb = t // SUBLANES
    ci, cw = col0_in // cb, col0_w // cb
    return pl.pallas_call(
        functools.partial(_conv_kernel, nl=nl),
        grid=(nb, nl, ncols // cb),
        in_specs=[
            pl.BlockSpec((SUBLANES, cb), lambda bb, i, c: (jnp.maximum((bb * nl + i) * rb - 1, 0), ci + c)),
            pl.BlockSpec((tl, cb), lambda bb, i, c: (bb * nl + i, ci + c)),
            pl.BlockSpec((SUBLANES, cb), lambda bb, i, c: (jnp.minimum((bb * nl + i + 1) * rb, nrb - 1), ci + c)),
            pl.BlockSpec((SSD_CONV_W, cb), lambda bb, i, c: (0, cw + c)),
            pl.BlockSpec((1, cb), lambda bb, i, c: (0, cw + c)),
        ],
        out_specs=pl.BlockSpec((tl, cb), lambda bb, i, c: (bb * nl + i, c)),
        out_shape=jax.ShapeDtypeStruct((t, ncols), out_dtype),
        compiler_params=_cparams("parallel", "parallel", "parallel"),
        name="conv_silu",
    )(proj, proj, proj, w, b)


def _ssd_kernel(xs_ref, b_ref, c_ref, dt_ref, alog_ref, y_ref, state_ref, *, reverse):
    q = SSD_CHUNK

    @pl.when(pl.program_id(1) == 0)
    def _():
        state_ref[...] = jnp.zeros(state_ref.shape, F32)

    lane0 = SSD_HEADS if reverse else 0
    row = lax.broadcasted_iota(jnp.int32, (q, q), 0)
    col = lax.broadcasted_iota(jnp.int32, (q, q), 1)
    mask = (row <= col) if reverse else (row >= col)
    tri = jnp.where(mask, 1.0, 0.0).astype(F32)

    dtc = dt_ref[...]
    a = dtc * (-jnp.exp(alog_ref[...]))
    c = jnp.dot(tri, a, precision=lax.Precision.HIGHEST, preferred_element_type=F32)
    c_t = c.T
    tot = c[0:1, :] if reverse else c[q - 1:q, :]
    ec = jnp.exp(c)
    ed = jnp.exp(tot - c)
    etot = jnp.exp(tot)

    for g in range(SSD_GROUPS):
        bg = b_ref[:, g * SSD_STATE:(g + 1) * SSD_STATE]
        cg = c_ref[:, g * SSD_STATE:(g + 1) * SSD_STATE]
        cb = lax.dot_general(cg, bg, (((1,), (1,)), ((), ())), preferred_element_type=F32)
        for r in range(SSD_HPG):
            h = g * SSD_HPG + r
            ln = lane0 + h
            hs = slice(h * SSD_HEAD_DIM, (h + 1) * SSD_HEAD_DIM)
            ps = slice(r * SSD_HEAD_DIM, (r + 1) * SSD_HEAD_DIM)
            seg = c[:, ln:ln + 1] - c_t[ln:ln + 1, :]
            lmat = jnp.exp(jnp.where(mask, seg, -jnp.inf))
            m = (cb * lmat).astype(BF16)
            xdt = xs_ref[:, hs] * dtc[:, ln:ln + 1]
            y_diag = jnp.dot(m, xdt.astype(BF16), preferred_element_type=F32)
            xw = (xdt * ed[:, ln:ln + 1]).astype(BF16)
            st_new = lax.dot_general(bg, xw, (((0,), (0,)), ((), ())), preferred_element_type=F32)
            prev = state_ref[g, :, ps]
            y_off = jnp.dot(cg, prev.astype(BF16), preferred_element_type=F32) * ec[:, ln:ln + 1]
            state_ref[g, :, ps] = prev * etot[:, ln:ln + 1] + st_new
            y_ref[:, hs] = y_diag + y_off


def _ssd_scan(xs, bc, dt, alog, nb, seq, reverse):
    t = xs.shape[0]
    q = SSD_CHUNK
    nc = seq // q

    def rblk(bb, j):
        return bb * nc + ((nc - 1 - j) if reverse else j)

    return pl.pallas_call(
        functools.partial(_ssd_kernel, reverse=reverse),
        grid=(nb, nc),
        in_specs=[
            pl.BlockSpec((q, SSD_INNER), lambda bb, j: (rblk(bb, j), 0)),
            pl.BlockSpec((q, SSD_GN), lambda bb, j: (rblk(bb, j), 0)),
            pl.BlockSpec((q, SSD_GN), lambda bb, j: (rblk(bb, j), 1)),
            pl.BlockSpec((q, LANES), lambda bb, j: (rblk(bb, j), 0)),
            pl.BlockSpec((1, LANES), lambda bb, j: (0, 0)),
        ],
        out_specs=pl.BlockSpec((q, SSD_INNER), lambda bb, j: (rblk(bb, j), 0)),
        out_shape=jax.ShapeDtypeStruct((t, SSD_INNER), F32),
        scratch_shapes=[pltpu.VMEM((SSD_GROUPS, SSD_STATE, SSD_HPG * SSD_HEAD_DIM), F32)],
        compiler_params=_cparams("parallel", "arbitrary"),
        name="ssd_scan_bwd" if reverse else "ssd_scan_fwd",
    )(xs, bc, bc, dt, alog)


def _ple_tail(x, mix, p_ref, wg_ref, wp_ref, gng_ref, png_ref, out_ref):
    x1 = x + mix
    gate = jax.nn.sigmoid(
        jnp.dot(_rms(x1, gng_ref[...]).astype(BF16), wg_ref[...], preferred_element_type=F32))
    e = _rms(jnp.dot(p_ref[...].astype(BF16), wp_ref[...], preferred_element_type=F32), png_ref[...])
    out_ref[...] = x1 + gate * e


def _ssd_post_kernel(yf_ref, yb_ref, xs_ref, z_ref, x_ref, p_ref, dsk_ref, ng_ref, wo_ref,
                     wg_ref, wp_ref, gng_ref, png_ref, out_ref):
    y = yf_ref[...] + yb_ref[...] + xs_ref[...] * dsk_ref[...]
    y = _rms(y * _silu(z_ref[...]), ng_ref[...])
    mix = jnp.dot(y.astype(BF16), wo_ref[...], preferred_element_type=F32)
    _ple_tail(x_ref[...], mix, p_ref, wg_ref, wp_ref, gng_ref, png_ref, out_ref)


def _att_post_kernel(o_ref, gate_ref, x_ref, p_ref, wo_ref, wg_ref, wp_ref, gng_ref, png_ref, out_ref):
    u = o_ref[...] * _silu(gate_ref[...])
    mix = jnp.dot(u.astype(BF16), wo_ref[...], preferred_element_type=F32)
    _ple_tail(x_ref[...], mix, p_ref, wg_ref, wp_ref, gng_ref, png_ref, out_ref)


def _row_spec(tm, width, colblk=0):
    return pl.BlockSpec((tm, width), lambda i: (i, colblk))


def _full_spec(shape):
    return pl.BlockSpec(shape, lambda i: (0,) * len(shape))


def _ssd_post(yf, yb, xs, proj, x, p, dsk, ng, wo, wg, wp, gng, png):
    t = x.shape[0]
    tm = _tile(t, 256)
    return pl.pallas_call(
        _ssd_post_kernel,
        grid=(t // tm,),
        in_specs=[
            _row_spec(tm, SSD_INNER), _row_spec(tm, SSD_INNER), _row_spec(tm, SSD_INNER),
            _row_spec(tm, SSD_INNER, 0),
            _row_spec(tm, D_MODEL), _row_spec(tm, PLE_DIM),
            _full_spec((1, SSD_INNER)), _full_spec((1, SSD_INNER)),
            _full_spec(wo.shape), _full_spec(wg.shape), _full_spec(wp.shape),
            _full_spec((1, D_MODEL)), _full_spec((1, D_MODEL)),
        ],
        out_specs=_row_spec(tm, D_MODEL),
        out_shape=jax.ShapeDtypeStruct((t, D_MODEL), F32),
        compiler_params=_cparams("parallel"),
        name="ssd_post",
    )(yf, yb, xs, proj, x, p, dsk, ng, wo, wg, wp, gng, png)


def _att_post(o, proj, x, p, wo, wg, wp, gng, png):
    t = x.shape[0]
    tm = _tile(t, 512)
    return pl.pallas_call(
        _att_post_kernel,
        grid=(t // tm,),
        in_specs=[
            _row_spec(tm, D_MODEL),
            _row_spec(tm, D_MODEL, 3),
            _row_spec(tm, D_MODEL), _row_spec(tm, PLE_DIM),
            _full_spec(wo.shape), _full_spec(wg.shape), _full_spec(wp.shape),
            _full_spec((1, D_MODEL)), _full_spec((1, D_MODEL)),
        ],
        out_specs=_row_spec(tm, D_MODEL),
        out_shape=jax.ShapeDtypeStruct((t, D_MODEL), F32),
        compiler_params=_cparams("parallel"),
        name="att_post",
    )(o, proj, x, p, wo, wg, wp, gng, png)


def _attprep_kernel(q_ref, k_ref, v_ref, qg_ref, kg_ref, qn_ref, kt_ref, vb_ref):
    shape = q_ref.shape
    first = lax.broadcasted_iota(jnp.int32, shape, 1) < ATT_QK_DIM

    def norm(x, g):
        x2 = x * x
        s1 = jnp.sum(jnp.where(first, x2, 0.0), axis=-1, keepdims=True)
        s2 = jnp.sum(jnp.where(first, 0.0, x2), axis=-1, keepdims=True)
        ms = jnp.where(first, s1, s2) * (1.0 / ATT_QK_DIM)
        return x * lax.rsqrt(ms + EPS) * g

    qn_ref[0] = (norm(q_ref[...], qg_ref[...]) * (ATT_QK_DIM ** -0.5)).astype(BF16)
    kt_ref[0] = norm(k_ref[...], kg_ref[...]).T.astype(BF16)
    vb_ref[0] = v_ref[...].astype(BF16)


def _att_prep(proj, qg, kg, nb, seq):
    tl = _tile(seq, 512)
    nl = seq // tl
    nh = ATT_HEADS
    blk = lambda off: pl.BlockSpec((tl, LANES), lambda bb, h, i: (bb * nl + i, off + h))
    return pl.pallas_call(
        _attprep_kernel,
        grid=(nb, nh, nl),
        in_specs=[blk(0), blk(nh), blk(2 * nh), _g3((1, LANES)), _g3((1, LANES))],
        out_specs=[
            pl.BlockSpec((1, tl, LANES), lambda bb, h, i: (bb * nh + h, i, 0)),
            pl.BlockSpec((1, LANES, tl), lambda bb, h, i: (bb * nh + h, 0, i)),
            pl.BlockSpec((1, tl, LANES), lambda bb, h, i: (bb * nh + h, i, 0)),
        ],
        out_shape=[
            jax.ShapeDtypeStruct((nb * nh, seq, LANES), BF16),
            jax.ShapeDtypeStruct((nb * nh, LANES, seq), BF16),
            jax.ShapeDtypeStruct((nb * nh, seq, LANES), BF16),
        ],
        compiler_params=_cparams("parallel", "parallel", "parallel"),
        name="att_prep",
    )(proj, proj, proj, qg, kg)


def _g3(shape):
    return pl.BlockSpec(shape, lambda bb, h, i: (0,) * len(shape))


def _flash_kernel(q_ref, kt_ref, v_ref, slope_ref, lam_ref, sg_ref, o_ref, *, tk, nk, lambda_init):
    tq = q_ref.shape[1]
    q = q_ref[0]
    first = lax.broadcasted_iota(jnp.int32, q.shape, 1) < ATT_QK_DIM
    zero = jnp.zeros_like(q)
    q1 = jnp.where(first, q, zero)
    q2 = jnp.where(first, zero, q)
    slope = slope_ref[0][:, 0:1]
    rel = (lax.broadcasted_iota(jnp.int32, (tq, tk), 1)
           - lax.broadcasted_iota(jnp.int32, (tq, tk), 0))
    q0 = pl.program_id(2) * tq

    def update(qm, kt, vt, bias, m, l, acc):
        s = jnp.dot(qm, kt, preferred_element_type=F32) - bias
        m_new = jnp.maximum(m, jnp.max(s, axis=-1, keepdims=True))
        alpha = jnp.exp(m - m_new)
        p = jnp.exp(s - m_new)
        l = alpha * l + jnp.sum(p, axis=-1, keepdims=True)
        acc = alpha * acc + jnp.dot(p.astype(BF16), vt, preferred_element_type=F32)
        return m_new, l, acc

    def body(kidx, carry):
        m1, l1, a1, m2, l2, a2 = carry
        k0 = pl.multiple_of(kidx * tk, tk)
        kt = kt_ref[0, :, pl.ds(k0, tk)]
        vt = v_ref[0, pl.ds(k0, tk), :]
        bias = jnp.abs(rel + (k0 - q0)).astype(F32) * slope
        m1, l1, a1 = update(q1, kt, vt, bias, m1, l1, a1)
        m2, l2, a2 = update(q2, kt, vt, bias, m2, l2, a2)
        return m1, l1, a1, m2, l2, a2

    neg = jnp.full((tq, 1), -jnp.inf, F32)
    zl = jnp.zeros((tq, 1), F32)
    za = jnp.zeros((tq, ATT_V_DIM), F32)
    m1, l1, a1, m2, l2, a2 = lax.fori_loop(0, nk, body, (neg, zl, za, neg, zl, za))

    lv = lam_ref[...]
    lam = (jnp.exp(jnp.sum(lv[0:1] * lv[1:2], axis=-1, keepdims=True))
           - jnp.exp(jnp.sum(lv[2:3] * lv[3:4], axis=-1, keepdims=True)) + lambda_init)
    o = a1 / l1 - lam * (a2 / l2)
    o_ref[...] = _rms(o, sg_ref[...]) * (1.0 - lambda_init)


def _flash(qn, kt, vb, slopes, lamv, sg, nb, seq, lambda_init):
    nh = ATT_HEADS
    tq = _tile(seq, 512)
    tk = _tile(seq, 512)
    nq = seq // tq
    return pl.pallas_call(
        functools.partial(_flash_kernel, tk=tk, nk=seq // tk, lambda_init=lambda_init),
        grid=(nb, nh, nq),
        in_specs=[
            pl.BlockSpec((1, tq, LANES), lambda bb, h, i: (bb * nh + h, i, 0)),
            pl.BlockSpec((1, LANES, seq), lambda bb, h, i: (bb * nh + h, 0, 0)),
            pl.BlockSpec((1, seq, LANES), lambda bb, h, i: (bb * nh + h, 0, 0)),
            pl.BlockSpec((1, 1, LANES), lambda bb, h, i: (h, 0, 0)),
            _g3((4, ATT_QK_DIM)),
            _g3((1, ATT_V_DIM)),
        ],
        out_specs=pl.BlockSpec((tq, LANES), lambda bb, h, i: (bb * nq + i, h)),
        out_shape=jax.ShapeDtypeStruct((nb * seq, nh * ATT_V_DIM), F32),
        compiler_params=_cparams("parallel", "parallel", "arbitrary"),
        name="flash_diff_attn",
    )(qn, kt, vb, slopes, lamv, sg)


def _alibi_slopes():
    s = np.array([2.0 ** (-8.0 * (i + 1) / ATT_HEADS) for i in range(ATT_HEADS)], dtype=np.float32)
    return jnp.asarray(np.broadcast_to(s[:, None, None], (ATT_HEADS, 1, LANES)).copy())


def _prep_weights(w):
    row = lambda v: v.reshape(1, -1).astype(F32)
    pad_lanes = lambda v: jnp.pad(v.reshape(1, -1).astype(F32), ((0, 0), (0, LANES - v.size)))
    out = {"layers": [], "slopes": _alibi_slopes()}
    for i in range(DEPTH):
        j = i // 2
        lw = {
            "pre_g": row(w["pre_norm_g"][i]),
            "wg": w["ple_w_gate"][i].astype(BF16),
            "wp": w["ple_w_proj"][i].astype(BF16),
            "gng": row(w["ple_gate_norm_g"][i]),
            "png": row(w["ple_norm_g"][i]),
        }
        if i % 2 == 0:
            win = w["ssd_w_in"][j]
            lw.update(
                w_main=win[:, :SSD_MAIN_COLS].astype(BF16),
                w_dt=jnp.pad(win[:, SSD_MAIN_COLS:], ((0, 0), (0, LANES - 2 * SSD_HEADS))).astype(BF16),
                dt_bias=pad_lanes(w["ssd_dt_bias"][j]),
                a_log=pad_lanes(w["ssd_a_log"][j]),
                conv_w=w["ssd_conv_w"][j].astype(F32),
                conv_b=row(w["ssd_conv_b"][j]),
                d_skip=row(jnp.repeat(w["ssd_d_skip"][j], SSD_HEAD_DIM)),
                norm_g=row(w["ssd_norm_g"][j]),
                w_out=w["ssd_w_out"][j].astype(BF16),
            )
        else:
            lw.update(
                w_in=w["att_w_in"][j].astype(BF16),
                qg=row(jnp.tile(w["att_q_norm_g"][j], 2)),
                kg=row(jnp.tile(w["att_k_norm_g"][j], 2)),
                lam=jnp.stack([w["att_lam_q1"][j], w["att_lam_k1"][j],
                               w["att_lam_q2"][j], w["att_lam_k2"][j]]).astype(F32),
                sub_g=row(w["att_sub_norm_g"][j]),
                w_out=w["att_w_out"][j].astype(BF16),
            )
        out["layers"].append(lw)
    return out


def _trunk(x3, p4, pw):
    nb, seq, d = x3.shape
    t = nb * seq
    x = x3.reshape(t, d)
    for i in range(DEPTH):
        lw = pw["layers"][i]
        p = p4[i].reshape(t, PLE_DIM)
        if i % 2 == 0:
            proj, dt = _inproj(x, lw["pre_g"], lw["w_main"], lw["w_dt"], lw["dt_bias"])
            xs = _conv_silu(proj, lw["conv_w"], lw["conv_b"], nb, seq,
                            SSD_INNER, 0, SSD_INNER, F32)
            bc = _conv_silu(proj, lw["conv_w"], lw["conv_b"], nb, seq,
                            2 * SSD_INNER, SSD_INNER, 2 * SSD_GN, BF16)
            yf = _ssd_scan(xs, bc, dt, lw["a_log"], nb, seq, False)
            yb = _ssd_scan(xs, bc, dt, lw["a_log"], nb, seq, True)
            x = _ssd_post(yf, yb, xs, proj, x, p, lw["d_skip"], lw["norm_g"], lw["w_out"],
                          lw["wg"], lw["wp"], lw["gng"], lw["png"])
        else:
            lambda_init = 0.8 - 0.6 * math.exp(-0.3 * i)
            proj = _inproj(x, lw["pre_g"], lw["w_in"])
            qn, kt, vb = _att_prep(proj, lw["qg"], lw["kg"], nb, seq)
            o = _flash(qn, kt, vb, pw["slopes"], lw["lam"], lw["sub_g"], nb, seq, lambda_init)
            x = _att_post(o, proj, x, p, lw["w_out"], lw["wg"], lw["wp"], lw["gng"], lw["png"])
    return x.reshape(nb, seq, d)


def kernel(x_prompt, x_sample, p_prompt, p_sample, pre_norm_g, ssd_w_in, ssd_conv_w, ssd_conv_b, ssd_dt_bias, ssd_a_log, ssd_d_skip, ssd_norm_g, ssd_w_out, att_w_in, att_q_norm_g, att_k_norm_g, att_lam_q1, att_lam_k1, att_lam_q2, att_lam_k2, att_sub_norm_g, att_w_out, ple_w_proj, ple_norm_g, ple_gate_norm_g, ple_w_gate):
    pw = _prep_weights(dict(
        pre_norm_g=pre_norm_g, ssd_w_in=ssd_w_in, ssd_conv_w=ssd_conv_w, ssd_conv_b=ssd_conv_b,
        ssd_dt_bias=ssd_dt_bias, ssd_a_log=ssd_a_log, ssd_d_skip=ssd_d_skip, ssd_norm_g=ssd_norm_g,
        ssd_w_out=ssd_w_out, att_w_in=att_w_in, att_q_norm_g=att_q_norm_g, att_k_norm_g=att_k_norm_g,
        att_lam_q1=att_lam_q1, att_lam_k1=att_lam_k1, att_lam_q2=att_lam_q2, att_lam_k2=att_lam_k2,
        att_sub_norm_g=att_sub_norm_g, att_w_out=att_w_out, ple_w_proj=ple_w_proj,
        ple_norm_g=ple_norm_g, ple_gate_norm_g=ple_gate_norm_g, ple_w_gate=ple_w_gate))
    return (_trunk(x_prompt, p_prompt, pw), _trunk(x_sample, p_sample, pw))
```

```python
import functools
import math

import numpy as np
import jax
import jax.numpy as jnp
from jax import lax
from jax.experimental import pallas as pl
from jax.experimental.pallas import tpu as pltpu

F32 = jnp.float32
BF16 = jnp.bfloat16

EPS = 1e-6
D_MODEL = 1024
DEPTH = 4
PLE_DIM = 256

SSD_INNER = 2048
SSD_HEAD_DIM = 64
SSD_HEADS = 32
SSD_GROUPS = 8
SSD_HPG = 4
SSD_STATE = 128
SSD_GN = SSD_GROUPS * SSD_STATE
SSD_CONV_W = 5
SSD_CHUNK = 128
SSD_MAIN_COLS = SSD_INNER + SSD_INNER + 2 * SSD_GN

ATT_HEADS = 8
ATT_QK_DIM = 64
ATT_V_DIM = 128
ATT_IN_COLS = 4096

LANES = 128
SUBLANES = 8
VMEM_LIMIT_BYTES = 56 * 1024 * 1024


def _cparams(*sem):
    return pltpu.CompilerParams(dimension_semantics=sem, vmem_limit_bytes=VMEM_LIMIT_BYTES)


def _tile(n, pref):
    t = min(n, pref)
    assert n % t == 0, (n, pref)
    return t


def _rms(x, g):
    return x * lax.rsqrt(jnp.mean(x * x, axis=-1, keepdims=True) + EPS) * g


def _silu(x):
    return x * jax.nn.sigmoid(x)


def _inproj_kernel(x_ref, g_ref, w_ref, *rest, has_dt):
    if has_dt:
        wdt_ref, dtb_ref, out_ref, dt_ref, hn_ref = rest
    else:
        out_ref, hn_ref = rest

    @pl.when(pl.program_id(1) == 0)
    def _():
        hn_ref[...] = _rms(x_ref[...], g_ref[...]).astype(BF16)
        if has_dt:
            raw = jnp.dot(hn_ref[...], wdt_ref[...], preferred_element_type=F32) + dtb_ref[...]
            dt_ref[...] = jax.nn.softplus(raw)

    out_ref[...] = jnp.dot(hn_ref[...], w_ref[...], preferred_element_type=F32)


def _inproj(x, g, w, wdt=None, dtb=None):
    t, d = x.shape
    n = w.shape[1]
    tm = _tile(t, 1024)
    tn = _tile(n, 2048)
    has_dt = wdt is not None
    in_specs = [
        pl.BlockSpec((tm, d), lambda i, j: (i, 0)),
        pl.BlockSpec((1, d), lambda i, j: (0, 0)),
        pl.BlockSpec((d, tn), lambda i, j: (0, j)),
    ]
    args = [x, g, w]
    out_shape = [jax.ShapeDtypeStruct((t, n), F32)]
    out_specs = [pl.BlockSpec((tm, tn), lambda i, j: (i, j))]
    if has_dt:
        in_specs += [pl.BlockSpec((d, LANES), lambda i, j: (0, 0)),
                     pl.BlockSpec((1, LANES), lambda i, j: (0, 0))]
        args += [wdt, dtb]
        out_shape.append(jax.ShapeDtypeStruct((t, LANES), F32))
        out_specs.append(pl.BlockSpec((tm, LANES), lambda i, j: (i, 0)))
    res = pl.pallas_call(
        functools.partial(_inproj_kernel, has_dt=has_dt),
        grid=(t // tm, n // tn),
        in_specs=in_specs,
        out_specs=out_specs,
        out_shape=out_shape,
        scratch_shapes=[pltpu.VMEM((tm, d), BF16)],
        compiler_params=_cparams("parallel", "arbitrary"),
        name="inproj_dt" if has_dt else "inproj",
    )(*args)
    return res if has_dt else res[0]


def _conv_kernel(prev_ref, cur_ref, next_ref, w_ref, b_ref, out_ref, *, nl):
    i = pl.program_id(1)
    tl = cur_ref.shape[0]
    prev = jnp.where(i > 0, prev_ref[...], 0.0)
    nxt = jnp.where(i < nl - 1, next_ref[...], 0.0)
    ext = jnp.concatenate([prev, cur_ref[...], nxt], axis=0)
    n = tl + 2 * SUBLANES
    acc = jnp.zeros(cur_ref.shape, F32) + b_ref[...]
    half = SSD_CONV_W // 2
    for k in range(SSD_CONV_W):
        shift = (half - k) % n
        r = ext if shift == 0 else pltpu.roll(ext, shift, 0)
        acc = acc + r[SUBLANES:SUBLANES + tl] * w_ref[k:k + 1, :]
    out_ref[...] = _silu(acc).astype(out_ref.dtype)


def _conv_silu(proj, w, b, nb, seq, col0_in, col0_w, ncols, out_dtype):
    t = proj.shape[0]
    cb = 512
    tl = _tile(seq, 512)
    nl = seq // tl
    rb = tl // SUBLANES
    nrb = t // SUBLANES
    ci, cw = col0_in // cb, col0_w // cb
    return pl.pallas_call(
        functools.partial(_conv_kernel, nl=nl),
        grid=(nb, nl, ncols // cb),
        in_specs=[
            pl.BlockSpec((SUBLANES, cb), lambda bb, i, c: (jnp.maximum((bb * nl + i) * rb - 1, 0), ci + c)),
            pl.BlockSpec((tl, cb), lambda bb, i, c: (bb * nl + i, ci + c)),
            pl.BlockSpec((SUBLANES, cb), lambda bb, i, c: (jnp.minimum((bb * nl + i + 1) * rb, nrb - 1), ci + c)),
            pl.BlockSpec((SSD_CONV_W, cb), lambda bb, i, c: (0, cw + c)),
            pl.BlockSpec((1, cb), lambda bb, i, c: (0, cw + c)),
        ],
        out_specs=pl.BlockSpec((tl, cb), lambda bb, i, c: (bb * nl + i, c)),
        out_shape=jax.ShapeDtypeStruct((t, ncols), out_dtype),
        compiler_params=_cparams("parallel", "parallel", "parallel"),
        name="conv_silu",
    )(proj, proj, proj, w, b)


def _ssd_tables(reverse):
    lane0 = SSD_HEADS if reverse else 0
    ee = np.zeros((SSD_GROUPS, 2 * LANES, SSD_HPG * SSD_HEAD_DIM), np.float32)
    sel = np.zeros((SSD_GROUPS * 2, 3 * LANES, 2 * LANES), np.float32)
    for g in range(SSD_GROUPS):
        for r in range(SSD_HPG):
            ln = lane0 + g * SSD_HPG + r
            for piece in range(2):
                ee[g, piece * LANES + ln, r * SSD_HEAD_DIM:(r + 1) * SSD_HEAD_DIM] = 1.0
            for piece in range(3):
                sel[2 * g + r // 2, piece * LANES + ln, (r % 2) * LANES:(r % 2 + 1) * LANES] = 1.0
    return jnp.asarray(ee, BF16), jnp.asarray(sel, BF16)


def _ssd_kernel(xs_ref, b_ref, c_ref, dt_ref, alog_ref, ee_ref, sel_ref, y_ref, state_ref, *, reverse):
    q = SSD_CHUNK
    gw = SSD_HPG * SSD_HEAD_DIM

    @pl.when(pl.program_id(1) == 0)
    def _():
        state_ref[...] = jnp.zeros(state_ref.shape, F32)

    lane0 = SSD_HEADS if reverse else 0
    row = lax.broadcasted_iota(jnp.int32, (q, q), 0)
    col = lax.broadcasted_iota(jnp.int32, (q, q), 1)
    mask = (row <= col) if reverse else (row >= col)
    tri = jnp.where(mask, 1.0, 0.0).astype(F32)
    head_of_lane = lax.broadcasted_iota(jnp.int32, (q, gw), 1) // SSD_HEAD_DIM

    dtc = dt_ref[...]
    a = dtc * (-jnp.exp(alog_ref[...]))
    c = jnp.dot(tri, a, precision=lax.Precision.HIGHEST, preferred_element_type=F32)
    c_t = c.T
    tot = c[0:1, :] if reverse else c[q - 1:q, :]
    ec = jnp.exp(c)
    ed = jnp.exp(tot - c)

    pieces2 = lambda x: jnp.concatenate(_split2(x), axis=1).astype(BF16)
    lhs_w = jnp.concatenate([pieces2(dtc), pieces2(dtc * ed), pieces2(ec)], axis=0)
    c3 = jnp.concatenate(_split3(c), axis=1).astype(BF16)

    for g in range(SSD_GROUPS):
        bg = b_ref[:, g * SSD_STATE:(g + 1) * SSD_STATE]
        cg = c_ref[:, g * SSD_STATE:(g + 1) * SSD_STATE]
        spread = jnp.dot(lhs_w, ee_ref[g], preferred_element_type=F32)
        dt_e, dted_e, ec_e = spread[0:q], spread[q:2 * q], spread[2 * q:3 * q]
        xs_g = xs_ref[:, g * gw:(g + 1) * gw]
        xdt = (xs_g * dt_e).astype(BF16)
        xw = (xs_g * dted_e).astype(BF16)
        cb = lax.dot_general(cg, bg, (((1,), (1,)), ((), ())), preferred_element_type=F32)
        bg_t = bg.astype(F32).T.astype(BF16)
        st_new = jnp.dot(bg_t, xw, preferred_element_type=F32)
        prev = state_ref[g]
        y_g = jnp.dot(cg, prev.astype(BF16), preferred_element_type=F32) * ec_e
        etot_e = ec_e[0:1, :] if reverse else ec_e[q - 1:q, :]
        state_ref[g] = prev * etot_e + st_new
        for pr in range(SSD_HPG // 2):
            cbc = jnp.dot(c3, sel_ref[2 * g + pr], preferred_element_type=F32)
            for k in range(2):
                r = 2 * pr + k
                ln = lane0 + g * SSD_HPG + r
                seg = cbc[:, k * LANES:(k + 1) * LANES] - c_t[ln:ln + 1, :]
                lmat = jnp.exp(jnp.where(mask, seg, -jnp.inf))
                m = (cb * lmat).astype(BF16)
                prod = jnp.dot(m, xdt, preferred_element_type=F32)
                y_g = y_g + jnp.where(head_of_lane == r, prod, 0.0)
        y_ref[:, g * gw:(g + 1) * gw] = y_g


def _ssd_scan(xs, bc, dt, alog, nb, seq, reverse):
    t = xs.shape[0]
    q = SSD_CHUNK
    nc = seq // q
    ee, sel = _ssd_tables(reverse)

    def rblk(bb, j):
        return bb * nc + ((nc - 1 - j) if reverse else j)

    return pl.pallas_call(
        functools.partial(_ssd_kernel, reverse=reverse),
        grid=(nb, nc),
        in_specs=[
            pl.BlockSpec((q, SSD_INNER), lambda bb, j: (rblk(bb, j), 0)),
            pl.BlockSpec((q, SSD_GN), lambda bb, j: (rblk(bb, j), 0)),
            pl.BlockSpec((q, SSD_GN), lambda bb, j: (rblk(bb, j), 1)),
            pl.BlockSpec((q, LANES), lambda bb, j: (rblk(bb, j), 0)),
            pl.BlockSpec((1, LANES), lambda bb, j: (0, 0)),
            pl.BlockSpec(ee.shape, lambda bb, j: (0, 0, 0)),
            pl.BlockSpec(sel.shape, lambda bb, j: (0, 0, 0)),
        ],
        out_specs=pl.BlockSpec((q, SSD_INNER), lambda bb, j: (rblk(bb, j), 0)),
        out_shape=jax.ShapeDtypeStruct((t, SSD_INNER), F32),
        scratch_shapes=[pltpu.VMEM((SSD_GROUPS, SSD_STATE, SSD_HPG * SSD_HEAD_DIM), F32)],
        compiler_params=_cparams("parallel", "arbitrary"),
        name="ssd_scan_bwd" if reverse else "ssd_scan_fwd",
    )(xs, bc, bc, dt, alog, ee, sel)


def _ple_tail(x, mix, p_ref, wg_ref, wp_ref, gng_ref, png_ref, out_ref):
    x1 = x + mix
    gate = jax.nn.sigmoid(
        jnp.dot(_rms(x1, gng_ref[...]).astype(BF16), wg_ref[...], preferred_element_type=F32))
    e = _rms(jnp.dot(p_ref[...].astype(BF16), wp_ref[...], preferred_element_type=F32), png_ref[...])
    out_ref[...] = x1 + gate * e


def _ssd_post_kernel(yf_ref, yb_ref, xs_ref, z_ref, x_ref, p_ref, dsk_ref, ng_ref, wo_ref,
                     wg_ref, wp_ref, gng_ref, png_ref, out_ref):
    y = yf_ref[...] + yb_ref[...] + xs_ref[...] * dsk_ref[...]
    y = _rms(y * _silu(z_ref[...]), ng_ref[...])
    mix = jnp.dot(y.astype(BF16), wo_ref[...], preferred_element_type=F32)
    _ple_tail(x_ref[...], mix, p_ref, wg_ref, wp_ref, gng_ref, png_ref, out_ref)


def _att_post_kernel(o_ref, gate_ref, x_ref, p_ref, wo_ref, wg_ref, wp_ref, gng_ref, png_ref, out_ref):
    u = o_ref[...] * _silu(gate_ref[...])
    mix = jnp.dot(u.astype(BF16), wo_ref[...], preferred_element_type=F32)
    _ple_tail(x_ref[...], mix, p_ref, wg_ref, wp_ref, gng_ref, png_ref, out_ref)


def _row_spec(tm, width, colblk=0):
    return pl.BlockSpec((tm, width), lambda i: (i, colblk))


def _full_spec(shape):
    return pl.BlockSpec(shape, lambda i: (0,) * len(shape))


def _ssd_post(yf, yb, xs, proj, x, p, dsk, ng, wo, wg, wp, gng, png):
    t = x.shape[0]
    tm = _tile(t, 256)
    return pl.pallas_call(
        _ssd_post_kernel,
        grid=(t // tm,),
        in_specs=[
            _row_spec(tm, SSD_INNER), _row_spec(tm, SSD_INNER), _row_spec(tm, SSD_INNER),
            _row_spec(tm, SSD_INNER, 0),
            _row_spec(tm, D_MODEL), _row_spec(tm, PLE_DIM),
            _full_spec((1, SSD_INNER)), _full_spec((1, SSD_INNER)),
            _full_spec(wo.shape), _full_spec(wg.shape), _full_spec(wp.shape),
            _full_spec((1, D_MODEL)), _full_spec((1, D_MODEL)),
        ],
        out_specs=_row_spec(tm, D_MODEL),
        out_shape=jax.ShapeDtypeStruct((t, D_MODEL), F32),
        compiler_params=_cparams("parallel"),
        name="ssd_post",
    )(yf, yb, xs, proj, x, p, dsk, ng, wo, wg, wp, gng, png)


def _att_post(o, proj, x, p, wo, wg, wp, gng, png):
    t = x.shape[0]
    tm = _tile(t, 512)
    return pl.pallas_call(
        _att_post_kernel,
        grid=(t // tm,),
        in_specs=[
            _row_spec(tm, D_MODEL),
            _row_spec(tm, D_MODEL, 3),
            _row_spec(tm, D_MODEL), _row_spec(tm, PLE_DIM),
            _full_spec(wo.shape), _full_spec(wg.shape), _full_spec(wp.shape),
            _full_spec((1, D_MODEL)), _full_spec((1, D_MODEL)),
        ],
        out_specs=_row_spec(tm, D_MODEL),
        out_shape=jax.ShapeDtypeStruct((t, D_MODEL), F32),
        compiler_params=_cparams("parallel"),
        name="att_post",
    )(o, proj, x, p, wo, wg, wp, gng, png)


def _attprep_kernel(q_ref, k_ref, v_ref, qg_ref, kg_ref, qn_ref, kt_ref, vb_ref):
    shape = q_ref.shape
    first = lax.broadcasted_iota(jnp.int32, shape, 1) < ATT_QK_DIM

    def norm(x, g):
        x2 = x * x
        s1 = jnp.sum(jnp.where(first, x2, 0.0), axis=-1, keepdims=True)
        s2 = jnp.sum(jnp.where(first, 0.0, x2), axis=-1, keepdims=True)
        ms = jnp.where(first, s1, s2) * (1.0 / ATT_QK_DIM)
        return x * lax.rsqrt(ms + EPS) * g

    qn_ref[0] = (norm(q_ref[...], qg_ref[...]) * (ATT_QK_DIM ** -0.5)).astype(BF16)
    kt_ref[0] = norm(k_ref[...], kg_ref[...]).T.astype(BF16)
    vb_ref[0] = v_ref[...].astype(BF16)


def _att_prep(proj, qg, kg, nb, seq):
    tl = _tile(seq, 512)
    nl = seq // tl
    nh = ATT_HEADS
    blk = lambda off: pl.BlockSpec((tl, LANES), lambda bb, h, i: (bb * nl + i, off + h))
    return pl.pallas_call(
        _attprep_kernel,
        grid=(nb, nh, nl),
        in_specs=[blk(0), blk(nh), blk(2 * nh), _g3((1, LANES)), _g3((1, LANES))],
        out_specs=[
            pl.BlockSpec((1, tl, LANES), lambda bb, h, i: (bb * nh + h, i, 0)),
            pl.BlockSpec((1, LANES, tl), lambda bb, h, i: (bb * nh + h, 0, i)),
            pl.BlockSpec((1, tl, LANES), lambda bb, h, i: (bb * nh + h, i, 0)),
        ],
        out_shape=[
            jax.ShapeDtypeStruct((nb * nh, seq, LANES), BF16),
            jax.ShapeDtypeStruct((nb * nh, LANES, seq), BF16),
            jax.ShapeDtypeStruct((nb * nh, seq, LANES), BF16),
        ],
        compiler_params=_cparams("parallel", "parallel", "parallel"),
        name="att_prep",
    )(proj, proj, proj, qg, kg)


def _g3(shape):
    return pl.BlockSpec(shape, lambda bb, h, i: (0,) * len(shape))


def _flash_kernel(q_ref, kt_ref, v_ref, slope_ref, lam_ref, sg_ref, o_ref, *, tk, nk, lambda_init):
    tq = q_ref.shape[1]
    q = q_ref[0]
    first = lax.broadcasted_iota(jnp.int32, q.shape, 1) < ATT_QK_DIM
    zero = jnp.zeros_like(q)
    q1 = jnp.where(first, q, zero)
    q2 = jnp.where(first, zero, q)
    slope = slope_ref[0][:, 0:1]
    rel = (lax.broadcasted_iota(jnp.int32, (tq, tk), 1)
           - lax.broadcasted_iota(jnp.int32, (tq, tk), 0))
    q0 = pl.program_id(2) * tq

    def update(qm, kt, vt, bias, m, l, acc):
        s = jnp.dot(qm, kt, preferred_element_type=F32) - bias
        m_new = jnp.maximum(m, jnp.max(s, axis=-1, keepdims=True))
        alpha = jnp.exp(m - m_new)
        p = jnp.exp(s - m_new)
        l = alpha * l + jnp.sum(p, axis=-1, keepdims=True)
        acc = alpha * acc + jnp.dot(p.astype(BF16), vt, preferred_element_type=F32)
        return m_new, l, acc

    def body(kidx, carry):
        m1, l1, a1, m2, l2, a2 = carry
        k0 = pl.multiple_of(kidx * tk, tk)
        kt = kt_ref[0, :, pl.ds(k0, tk)]
        vt = v_ref[0, pl.ds(k0, tk), :]
        bias = jnp.abs(rel + (k0 - q0)).astype(F32) * slope
        m1, l1, a1 = update(q1, kt, vt, bias, m1, l1, a1)
        m2, l2, a2 = update(q2, kt, vt, bias, m2, l2, a2)
        return m1, l1, a1, m2, l2, a2

    neg = jnp.full((tq, 1), -jnp.inf, F32)
    zl = jnp.zeros((tq, 1), F32)
    za = jnp.zeros((tq, ATT_V_DIM), F32)
    m1, l1, a1, m2, l2, a2 = lax.fori_loop(0, nk, body, (neg, zl, za, neg, zl, za))

    lv = lam_ref[...]
    lam = (jnp.exp(jnp.sum(lv[0:1] * lv[1:2], axis=-1, keepdims=True))
           - jnp.exp(jnp.sum(lv[2:3] * lv[3:4], axis=-1, keepdims=True)) + lambda_init)
    o = a1 / l1 - lam * (a2 / l2)
    o_ref[...] = _rms(o, sg_ref[...]) * (1.0 - lambda_init)


def _flash(qn, kt, vb, slopes, lamv, sg, nb, seq, lambda_init):
    nh = ATT_HEADS
    tq = _tile(seq, 512)
    tk = _tile(seq, 512)
    nq = seq // tq
    return pl.pallas_call(
        functools.partial(_flash_kernel, tk=tk, nk=seq // tk, lambda_init=lambda_init),
        grid=(nb, nh, nq),
        in_specs=[
            pl.BlockSpec((1, tq, LANES), lambda bb, h, i: (bb * nh + h, i, 0)),
            pl.BlockSpec((1, LANES, seq), lambda bb, h, i: (bb * nh + h, 0, 0)),
            pl.BlockSpec((1, seq, LANES), lambda bb, h, i: (bb * nh + h, 0, 0)),
            pl.BlockSpec((1, 1, LANES), lambda bb, h, i: (h, 0, 0)),
            _g3((4, ATT_QK_DIM)),
            _g3((1, ATT_V_DIM)),
        ],
        out_specs=pl.BlockSpec((tq, LANES), lambda bb, h, i: (bb * nq + i, h)),
        out_shape=jax.ShapeDtypeStruct((nb * seq, nh * ATT_V_DIM), F32),
        compiler_params=_cparams("parallel", "parallel", "arbitrary"),
        name="flash_diff_attn",
    )(qn, kt, vb, slopes, lamv, sg)


FLASH_TILE = 512
POS_U = 0
POS_W_LOW = 2
POS_OFF = 4
POS_SHIFT = 7
POS_W_UP = 8
BAND_EXP_CUTOFF = 112.0
SHIFT_SAFE_MAX = 40.0


def _split2(x):
    hi = x.astype(BF16).astype(F32)
    lo = (x - hi).astype(BF16).astype(F32)
    return hi, lo


def _split3(x):
    hi = x.astype(BF16).astype(F32)
    r = x - hi
    mid = r.astype(BF16).astype(F32)
    lo = (r - mid).astype(BF16).astype(F32)
    return hi, mid, lo


def _attprep_band_kernel(q_ref, k_ref, v_ref, qg_ref, kg_ref, slope_ref, qn_ref, kt_ref, vb_ref):
    tl = q_ref.shape[0]
    first = lax.broadcasted_iota(jnp.int32, (tl, LANES), 1) < ATT_QK_DIM

    def norm(x, g):
        x2 = x * x
        s1 = jnp.sum(jnp.where(first, x2, 0.0), axis=-1, keepdims=True)
        s2 = jnp.sum(jnp.where(first, 0.0, x2), axis=-1, keepdims=True)
        ms = jnp.where(first, s1, s2) * (1.0 / ATT_QK_DIM)
        return x * lax.rsqrt(ms + EPS) * g

    qn_ref[0] = (norm(q_ref[...], qg_ref[...]) * (ATT_QK_DIM ** -0.5)).astype(BF16)
    kt_ref[0, 0:LANES, :] = norm(k_ref[...], kg_ref[...]).T.astype(BF16)

    mq = jnp.max(jnp.abs(qg_ref[...]), axis=-1, keepdims=True)
    mk = jnp.max(jnp.abs(kg_ref[...]), axis=-1, keepdims=True)
    shift = mq * mk * (ATT_QK_DIM ** 0.5 * 1.02)

    slope = slope_ref[0][:, 0:1]
    r = lax.broadcasted_iota(jnp.int32, (LANES, tl), 0)
    j = lax.broadcasted_iota(jnp.int32, (LANES, tl), 1).astype(F32)
    wl_hi, wl_lo = _split2(slope * (tl - 1.0 - j))
    wu_hi, wu_lo = _split2(slope * j)
    minus_one = (r == POS_U) | (r == POS_U + 1) | ((r >= POS_OFF) & (r < POS_OFF + 3))
    pos = jnp.where(minus_one, -1.0, 0.0)
    pos = jnp.where(r == POS_W_LOW, -wl_hi, pos)
    pos = jnp.where(r == POS_W_LOW + 1, -wl_lo, pos)
    pos = jnp.where(r == POS_SHIFT, -shift, pos)
    pos = jnp.where(r == POS_W_UP, -wu_hi, pos)
    pos = jnp.where(r == POS_W_UP + 1, -wu_lo, pos)
    kt_ref[0, LANES:2 * LANES, :] = pos.astype(BF16)

    lane = lax.broadcasted_iota(jnp.int32, (tl, LANES), 1)
    vb_ref[0, :, 0:LANES] = v_ref[...].astype(BF16)
    vb_ref[0, :, LANES:2 * LANES] = jnp.where(lane == 0, 1.0, 0.0).astype(BF16)


def _att_prep_band(proj, qg, kg, slopes, nb, seq):
    tl = _tile(seq, FLASH_TILE)
    nl = seq // tl
    nh = ATT_HEADS
    blk = lambda off: pl.BlockSpec((tl, LANES), lambda bb, h, i: (bb * nl + i, off + h))
    return pl.pallas_call(
        _attprep_band_kernel,
        grid=(nb, nh, nl),
        in_specs=[blk(0), blk(nh), blk(2 * nh), _g3((1, LANES)), _g3((1, LANES)),
                  pl.BlockSpec((1, 1, LANES), lambda bb, h, i: (h, 0, 0))],
        out_specs=[
            pl.BlockSpec((1, tl, LANES), lambda bb, h, i: (bb * nh + h, i, 0)),
            pl.BlockSpec((1, 2 * LANES, tl), lambda bb, h, i: (bb * nh + h, 0, i)),
            pl.BlockSpec((1, tl, 2 * LANES), lambda bb, h, i: (bb * nh + h, i, 0)),
        ],
        out_shape=[
            jax.ShapeDtypeStruct((nb * nh, seq, LANES), BF16),
            jax.ShapeDtypeStruct((nb * nh, 2 * LANES, seq), BF16),
            jax.ShapeDtypeStruct((nb * nh, seq, 2 * LANES), BF16),
        ],
        compiler_params=_cparams("parallel", "parallel", "parallel"),
        name="att_prep_band",
    )(proj, proj, proj, qg, kg, slopes)


def _flash_band_kernel(q_ref, kt_ref, v_ref, slope_ref, lam_ref, sg_ref, o_ref, lhs_ref, acc_ref,
                       *, tk, nk, lambda_init):
    tq = q_ref.shape[1]
    assert tq == tk
    h = pl.program_id(1)
    qi = pl.program_id(2)
    q0 = qi * tq
    slope_row = slope_ref[0]
    lane = lax.broadcasted_iota(jnp.int32, (tq, LANES), 1)
    rowf = lax.broadcasted_iota(jnp.int32, (tq, LANES), 0).astype(F32)

    q = q_ref[0]
    zero = jnp.zeros_like(q)
    first = lane < ATT_QK_DIM
    lhs_ref[0:tq, 0:LANES] = jnp.where(first, q, zero)
    lhs_ref[tq:2 * tq, 0:LANES] = jnp.where(first, zero, q)
    acc_ref[...] = jnp.zeros(acc_ref.shape, F32)

    def pos_base(u, w_lane):
        u_hi, u_lo = _split2(u)
        ones = (lane == POS_SHIFT) | (lane == w_lane) | (lane == w_lane + 1)
        base = jnp.where(ones, 1.0, 0.0)
        base = jnp.where(lane == POS_U, u_hi, base)
        return jnp.where(lane == POS_U + 1, u_lo, base)

    off_lanes = (lane >= POS_OFF) & (lane < POS_OFF + 3)
    lane1 = lax.broadcasted_iota(jnp.int32, (2 * SUBLANES, LANES), 1)
    slope_tile = jnp.broadcast_to(slope_row, (2 * SUBLANES, LANES))

    def set_pos(base, dist):
        if dist is not None:
            hi, mid, lo = _split3(slope_tile * dist.astype(F32))
            orow = jnp.where(lane1 == POS_OFF, hi, jnp.where(lane1 == POS_OFF + 1, mid, lo))
            base = jnp.where(off_lanes, orow[0:1, :], base)
        pos = base.astype(BF16)
        lhs_ref[0:tq, LANES:2 * LANES] = pos
        lhs_ref[tq:2 * tq, LANES:2 * LANES] = pos

    def accumulate(kidx, bias):
        k0 = pl.multiple_of(kidx * tk, tk)
        s = jnp.dot(lhs_ref[...], kt_ref[0, :, pl.ds(k0, tk)], preferred_element_type=F32)
        if bias is not None:
            s = s - bias
        p = jnp.exp(s).astype(BF16)
        acc_ref[...] += jnp.dot(p, v_ref[0, pl.ds(k0, tk), :], preferred_element_type=F32)

    band = lax.shift_left(jnp.int32(int(BAND_EXP_CUTOFF)), h + 1)
    k_lo = jnp.maximum(q0 + 1 - band, 0) // tk
    k_hi = jnp.minimum((q0 + tq - 2 + band + tk) // tk, nk)

    base_low = pos_base(slope_row * rowf, POS_W_LOW)

    def low_body(kidx, carry):
        set_pos(base_low, q0 - kidx * tk - (tk - 1))
        accumulate(kidx, None)
        return carry

    lax.fori_loop(k_lo, qi, low_body, 0)

    base_diag = jnp.where(lane == POS_SHIFT, 1.0, 0.0)
    set_pos(base_diag, None)
    rel = (lax.broadcasted_iota(jnp.int32, (tq, tk), 1)
           - lax.broadcasted_iota(jnp.int32, (tq, tk), 0))
    bias = jnp.abs(rel).astype(F32) * slope_row[:, 0:1]
    accumulate(qi, jnp.concatenate([bias, bias], axis=0))

    base_up = pos_base(slope_row * (tq - 1.0 - rowf), POS_W_UP)

    def up_body(kidx, carry):
        set_pos(base_up, kidx * tk - q0 - (tq - 1))
        accumulate(kidx, None)
        return carry

    lax.fori_loop(qi + 1, k_hi, up_body, 0)

    lv = lam_ref[...]
    lam = (jnp.exp(jnp.sum(lv[0:1] * lv[1:2], axis=-1, keepdims=True))
           - jnp.exp(jnp.sum(lv[2:3] * lv[3:4], axis=-1, keepdims=True)) + lambda_init)
    a1 = acc_ref[0:tq, 0:LANES]
    l1 = acc_ref[0:tq, LANES:LANES + 1]
    a2 = acc_ref[tq:2 * tq, 0:LANES]
    l2 = acc_ref[tq:2 * tq, LANES:LANES + 1]
    o = a1 / l1 - lam * (a2 / l2)
    o_ref[...] = _rms(o, sg_ref[...]) * (1.0 - lambda_init)


def _flash_band(qn, kt, vb, slopes, lamv, sg, nb, seq, lambda_init):
    nh = ATT_HEADS
    tq = _tile(seq, FLASH_TILE)
    nq = seq // tq
    return pl.pallas_call(
        functools.partial(_flash_band_kernel, tk=tq, nk=nq, lambda_init=lambda_init),
        grid=(nb, nh, nq),
        in_specs=[
            pl.BlockSpec((1, tq, LANES), lambda bb, h, i: (bb * nh + h, i, 0)),
            pl.BlockSpec((1, 2 * LANES, seq), lambda bb, h, i: (bb * nh + h, 0, 0)),
            pl.BlockSpec((1, seq, 2 * LANES), lambda bb, h, i: (bb * nh + h, 0, 0)),
            pl.BlockSpec((1, 1, LANES), lambda bb, h, i: (h, 0, 0)),
            _g3((4, ATT_QK_DIM)),
            _g3((1, ATT_V_DIM)),
        ],
        out_specs=pl.BlockSpec((tq, LANES), lambda bb, h, i: (bb * nq + i, h)),
        out_shape=jax.ShapeDtypeStruct((nb * seq, nh * ATT_V_DIM), F32),
        scratch_shapes=[pltpu.VMEM((2 * tq, 2 * LANES), BF16), pltpu.VMEM((2 * tq, 2 * LANES), F32)],
        compiler_params=_cparams("parallel", "parallel", "arbitrary"),
        name="flash_band",
    )(qn, kt, vb, slopes, lamv, sg)


def _attention(proj, lw, slopes, nb, seq, lambda_init):
    def banded(proj):
        qn, kt, vb = _att_prep_band(proj, lw["qg"], lw["kg"], slopes, nb, seq)
        return _flash_band(qn, kt, vb, slopes, lw["lam"], lw["sub_g"], nb, seq, lambda_init)

    def online(proj):
        qn, kt, vb = _att_prep(proj, lw["qg"], lw["kg"], nb, seq)
        return _flash(qn, kt, vb, slopes, lw["lam"], lw["sub_g"], nb, seq, lambda_init)

    shift = jnp.max(jnp.abs(lw["qg"])) * jnp.max(jnp.abs(lw["kg"])) * (ATT_QK_DIM ** 0.5 * 1.02)
    return lax.cond(shift < SHIFT_SAFE_MAX, banded, online, proj)


def _alibi_slopes():
    s = np.array([2.0 ** (-8.0 * (i + 1) / ATT_HEADS) for i in range(ATT_HEADS)], dtype=np.float32)
    return jnp.asarray(np.broadcast_to(s[:, None, None], (ATT_HEADS, 1, LANES)).copy())


def _prep_weights(w):
    row = lambda v: v.reshape(1, -1).astype(F32)
    pad_lanes = lambda v: jnp.pad(v.reshape(1, -1).astype(F32), ((0, 0), (0, LANES - v.size)))
    out = {"layers": [], "slopes": _alibi_slopes()}
    for i in range(DEPTH):
        j = i // 2
        lw = {
            "pre_g": row(w["pre_norm_g"][i]),
            "wg": w["ple_w_gate"][i].astype(BF16),
            "wp": w["ple_w_proj"][i].astype(BF16),
            "gng": row(w["ple_gate_norm_g"][i]),
            "png": row(w["ple_norm_g"][i]),
        }
        if i % 2 == 0:
            win = w["ssd_w_in"][j]
            lw.update(
                w_main=win[:, :SSD_MAIN_COLS].astype(BF16),
                w_dt=jnp.pad(win[:, SSD_MAIN_COLS:], ((0, 0), (0, LANES - 2 * SSD_HEADS))).astype(BF16),
                dt_bias=pad_lanes(w["ssd_dt_bias"][j]),
                a_log=pad_lanes(w["ssd_a_log"][j]),
                conv_w=w["ssd_conv_w"][j].astype(F32),
                conv_b=row(w["ssd_conv_b"][j]),
                d_skip=row(jnp.repeat(w["ssd_d_skip"][j], SSD_HEAD_DIM)),
                norm_g=row(w["ssd_norm_g"][j]),
                w_out=w["ssd_w_out"][j].astype(BF16),
            )
        else:
            lw.update(
                w_in=w["att_w_in"][j].astype(BF16),
                qg=row(jnp.tile(w["att_q_norm_g"][j], 2)),
                kg=row(jnp.tile(w["att_k_norm_g"][j], 2)),
                lam=jnp.stack([w["att_lam_q1"][j], w["att_lam_k1"][j],
                               w["att_lam_q2"][j], w["att_lam_k2"][j]]).astype(F32),
                sub_g=row(w["att_sub_norm_g"][j]),
                w_out=w["att_w_out"][j].astype(BF16),
            )
        out["layers"].append(lw)
    return out


def _trunk(x3, p4, pw):
    nb, seq, d = x3.shape
    t = nb * seq
    x = x3.reshape(t, d)
    for i in range(DEPTH):
        lw = pw["layers"][i]
        p = p4[i].reshape(t, PLE_DIM)
        if i % 2 == 0:
            proj, dt = _inproj(x, lw["pre_g"], lw["w_main"], lw["w_dt"], lw["dt_bias"])
            xs = _conv_silu(proj, lw["conv_w"], lw["conv_b"], nb, seq,
                            SSD_INNER, 0, SSD_INNER, F32)
            bc = _conv_silu(proj, lw["conv_w"], lw["conv_b"], nb, seq,
                            2 * SSD_INNER, SSD_INNER, 2 * SSD_GN, BF16)
            yf = _ssd_scan(xs, bc, dt, lw["a_log"], nb, seq, False)
            yb = _ssd_scan(xs, bc, dt, lw["a_log"], nb, seq, True)
            x = _ssd_post(yf, yb, xs, proj, x, p, lw["d_skip"], lw["norm_g"], lw["w_out"],
                          lw["wg"], lw["wp"], lw["gng"], lw["png"])
        else:
            lambda_init = 0.8 - 0.6 * math.exp(-0.3 * i)
            proj = _inproj(x, lw["pre_g"], lw["w_in"])
            o = _attention(proj, lw, pw["slopes"], nb, seq, lambda_init)
            x = _att_post(o, proj, x, p, lw["w_out"], lw["wg"], lw["wp"], lw["gng"], lw["png"])
    return x.reshape(nb, seq, d)


def kernel(x_prompt, x_sample, p_prompt, p_sample, pre_norm_g, ssd_w_in, ssd_conv_w, ssd_conv_b, ssd_dt_bias, ssd_a_log, ssd_d_skip, ssd_norm_g, ssd_w_out, att_w_in, att_q_norm_g, att_k_norm_g, att_lam_q1, att_lam_k1, att_lam_q2, att_lam_k2, att_sub_norm_g, att_w_out, ple_w_proj, ple_norm_g, ple_gate_norm_g, ple_w_gate):
    pw = _prep_weights(dict(
        pre_norm_g=pre_norm_g, ssd_w_in=ssd_w_in, ssd_conv_w=ssd_conv_w, ssd_conv_b=ssd_conv_b,
        ssd_dt_bias=ssd_dt_bias, ssd_a_log=ssd_a_log, ssd_d_skip=ssd_d_skip, ssd_norm_g=ssd_norm_g,
        ssd_w_out=ssd_w_out, att_w_in=att_w_in, att_q_norm_g=att_q_norm_g, att_k_norm_g=att_k_norm_g,
        att_lam_q1=att_lam_q1, att_lam_k1=att_lam_k1, att_lam_q2=att_lam_q2, att_lam_k2=att_lam_k2,
        att_sub_norm_g=att_sub_norm_g, att_w_out=att_w_out, ple_w_proj=ple_w_proj,
        ple_norm_g=ple_norm_g, ple_gate_norm_g=ple_gate_norm_g, ple_w_gate=ple_w_gate))
    return (_trunk(x_prompt, p_prompt, pw), _trunk(x_sample, p_sample, pw))
```

```python
import functools
import math

import numpy as np
import jax
import jax.numpy as jnp
from jax import lax
from jax.experimental import pallas as pl
from jax.experimental.pallas import tpu as pltpu

F32 = jnp.float32
BF16 = jnp.bfloat16

EPS = 1e-6
D_MODEL = 1024
DEPTH = 4
PLE_DIM = 256

SSD_INNER = 2048
SSD_HEAD_DIM = 64
SSD_HEADS = 32
SSD_GROUPS = 8
SSD_HPG = 4
SSD_STATE = 128
SSD_GN = SSD_GROUPS * SSD_STATE
SSD_CONV_W = 5
SSD_CHUNK = 128
SSD_MAIN_COLS = SSD_INNER + SSD_INNER + 2 * SSD_GN

ATT_HEADS = 8
ATT_QK_DIM = 64
ATT_V_DIM = 128
ATT_IN_COLS = 4096

LANES = 128
SUBLANES = 8
VMEM_LIMIT_BYTES = 56 * 1024 * 1024


def _cparams(*sem):
    return pltpu.CompilerParams(dimension_semantics=sem, vmem_limit_bytes=VMEM_LIMIT_BYTES)


def _tile(n, pref):
    t = min(n, pref)
    assert n % t == 0, (n, pref)
    return t


def _rms(x, g):
    return x * lax.rsqrt(jnp.mean(x * x, axis=-1, keepdims=True) + EPS) * g


def _silu(x):
    return x * jax.nn.sigmoid(x)


def _inproj_kernel(x_ref, g_ref, w_ref, *rest, has_dt):
    if has_dt:
        wdt_ref, dtb_ref, out_ref, dt_ref, hn_ref = rest
    else:
        out_ref, hn_ref = rest

    @pl.when(pl.program_id(1) == 0)
    def _():
        hn_ref[...] = _rms(x_ref[...], g_ref[...]).astype(BF16)
        if has_dt:
            raw = jnp.dot(hn_ref[...], wdt_ref[...], preferred_element_type=F32) + dtb_ref[...]
            dt_ref[...] = jax.nn.softplus(raw)

    out_ref[...] = jnp.dot(hn_ref[...], w_ref[...], preferred_element_type=F32)


def _inproj(x, g, w, wdt=None, dtb=None):
    t, d = x.shape
    n = w.shape[1]
    tm = _tile(t, 1024)
    tn = _tile(n, 2048)
    has_dt = wdt is not None
    in_specs = [
        pl.BlockSpec((tm, d), lambda i, j: (i, 0)),
        pl.BlockSpec((1, d), lambda i, j: (0, 0)),
        pl.BlockSpec((d, tn), lambda i, j: (0, j)),
    ]
    args = [x, g, w]
    out_shape = [jax.ShapeDtypeStruct((t, n), F32)]
    out_specs = [pl.BlockSpec((tm, tn), lambda i, j: (i, j))]
    if has_dt:
        in_specs += [pl.BlockSpec((d, LANES), lambda i, j: (0, 0)),
                     pl.BlockSpec((1, LANES), lambda i, j: (0, 0))]
        args += [wdt, dtb]
        out_shape.append(jax.ShapeDtypeStruct((t, LANES), F32))
        out_specs.append(pl.BlockSpec((tm, LANES), lambda i, j: (i, 0)))
    res = pl.pallas_call(
        functools.partial(_inproj_kernel, has_dt=has_dt),
        grid=(t // tm, n // tn),
        in_specs=in_specs,
        out_specs=out_specs,
        out_shape=out_shape,
        scratch_shapes=[pltpu.VMEM((tm, d), BF16)],
        compiler_params=_cparams("parallel", "arbitrary"),
        name="inproj_dt" if has_dt else "inproj",
    )(*args)
    return res if has_dt else res[0]


def _conv_kernel(prev_ref, cur_ref, next_ref, w_ref, b_ref, out_ref, *, nl):
    i = pl.program_id(1)
    tl = cur_ref.shape[0]
    prev = jnp.where(i > 0, prev_ref[...], 0.0)
    nxt = jnp.where(i < nl - 1, next_ref[...], 0.0)
    ext = jnp.concatenate([prev, cur_ref[...], nxt], axis=0)
    n = tl + 2 * SUBLANES
    acc = jnp.zeros(cur_ref.shape, F32) + b_ref[...]
    half = SSD_CONV_W // 2
    for k in range(SSD_CONV_W):
        shift = (half - k) % n
        r = ext if shift == 0 else pltpu.roll(ext, shift, 0)
        acc = acc + r[SUBLANES:SUBLANES + tl] * w_ref[k:k + 1, :]
    out_ref[...] = _silu(acc).astype(out_ref.dtype)


def _conv_silu(proj, w, b, nb, seq, col0_in, col0_w, ncols, out_dtype):
    t = proj.shape[0]
    cb = 512
    tl = _tile(seq, 512)
    nl = seq // tl
    rb = tl // SUBLANES
    nrb = t // SUBLANES
    ci, cw = col0_in // cb, col0_w // cb
    return pl.pallas_call(
        functools.partial(_conv_kernel, nl=nl),
        grid=(nb, nl, ncols // cb),
        in_specs=[
            pl.BlockSpec((SUBLANES, cb), lambda bb, i, c: (jnp.maximum((bb * nl + i) * rb - 1, 0), ci + c)),
            pl.BlockSpec((tl, cb), lambda bb, i, c: (bb * nl + i, ci + c)),
            pl.BlockSpec((SUBLANES, cb), lambda bb, i, c: (jnp.minimum((bb * nl + i + 1) * rb, nrb - 1), ci + c)),
            pl.BlockSpec((SSD_CONV_W, cb), lambda bb, i, c: (0, cw + c)),
            pl.BlockSpec((1, cb), lambda bb, i, c: (0, cw + c)),
        ],
        out_specs=pl.BlockSpec((tl, cb), lambda bb, i, c: (bb * nl + i, c)),
        out_shape=jax.ShapeDtypeStruct((t, ncols), out_dtype),
        compiler_params=_cparams("parallel", "parallel", "parallel"),
        name="conv_silu",
    )(proj, proj, proj, w, b)


def _ssd_tables(reverse):
    lane0 = SSD_HEADS if reverse else 0
    ee = np.zeros((2 * LANES, SSD_INNER), np.float32)
    sel = np.zeros((3 * LANES, SSD_HEADS * LANES), np.float32)
    for h in range(SSD_HEADS):
        for piece in range(2):
            ee[piece * LANES + lane0 + h, h * SSD_HEAD_DIM:(h + 1) * SSD_HEAD_DIM] = 1.0
        for piece in range(3):
            sel[piece * LANES + lane0 + h, h * LANES:(h + 1) * LANES] = 1.0
    return jnp.asarray(ee, BF16), jnp.asarray(sel, BF16)


def _ssd_kernel(xs_ref, b_ref, c_ref, dt_ref, alog_ref, ee_ref, sel_ref, *rest, reverse):
    q = SSD_CHUNK
    gw = SSD_HPG * SSD_HEAD_DIM
    if reverse:
        yf_ref, dsk_ref, y_ref, state_ref = rest
    else:
        y_ref, state_ref = rest

    @pl.when(pl.program_id(1) == 0)
    def _():
        state_ref[...] = jnp.zeros(state_ref.shape, F32)

    lane0 = SSD_HEADS if reverse else 0
    row = lax.broadcasted_iota(jnp.int32, (q, q), 0)
    col = lax.broadcasted_iota(jnp.int32, (q, q), 1)
    mask = (row <= col) if reverse else (row >= col)
    tri = jnp.where(mask, 1.0, 0.0).astype(F32)
    head_of_lane = lax.broadcasted_iota(jnp.int32, (q, gw), 1) // SSD_HEAD_DIM

    dtc = dt_ref[...]
    a = dtc * (-jnp.exp(alog_ref[...]))
    c = jnp.dot(tri, a, precision=lax.Precision.HIGHEST, preferred_element_type=F32)
    c_t = c.T
    tot = c[0:1, :] if reverse else c[q - 1:q, :]
    ec = jnp.exp(c)
    ed = jnp.exp(tot - c)

    pieces2 = lambda x: jnp.concatenate(_split2(x), axis=1).astype(BF16)
    lhs_w = jnp.concatenate([pieces2(dtc), pieces2(dtc * ed), pieces2(ec)], axis=0)
    spread = jnp.dot(lhs_w, ee_ref[...], preferred_element_type=F32)
    c3 = jnp.concatenate(_split3(c), axis=1).astype(BF16)
    c_cols = jnp.dot(c3, sel_ref[...], preferred_element_type=F32)

    for g in range(SSD_GROUPS):
        gs = slice(g * gw, (g + 1) * gw)
        bg = b_ref[:, g * SSD_STATE:(g + 1) * SSD_STATE]
        cg = c_ref[:, g * SSD_STATE:(g + 1) * SSD_STATE]
        dt_e, dted_e, ec_e = spread[0:q, gs], spread[q:2 * q, gs], spread[2 * q:3 * q, gs]
        xs_g = xs_ref[:, gs]
        xdt = (xs_g * dt_e).astype(BF16)
        xw = (xs_g * dted_e).astype(BF16)
        cb = lax.dot_general(cg, bg, (((1,), (1,)), ((), ())), preferred_element_type=F32)
        bg_t = bg.astype(F32).T.astype(BF16)
        st_new = jnp.dot(bg_t, xw, preferred_element_type=F32)
        prev = state_ref[g]
        y_off = jnp.dot(cg, prev.astype(BF16), preferred_element_type=F32) * ec_e
        etot_e = ec_e[0:1, :] if reverse else ec_e[q - 1:q, :]
        state_ref[g] = prev * etot_e + st_new
        ms, xd = [], []
        for r in range(SSD_HPG):
            h = g * SSD_HPG + r
            seg = c_cols[:, h * LANES:(h + 1) * LANES] - c_t[lane0 + h:lane0 + h + 1, :]
            lmat = jnp.exp(jnp.where(mask, seg, -jnp.inf))
            ms.append((cb * lmat).astype(BF16))
            xd.append(jnp.where(head_of_lane == r, xdt, jnp.zeros_like(xdt)))
        y_diag = jnp.dot(jnp.concatenate(ms, axis=1), jnp.concatenate(xd, axis=0),
                         preferred_element_type=F32)
        y_g = y_diag + y_off
        if reverse:
            y_g = yf_ref[:, gs] + y_g + xs_g * dsk_ref[:, gs]
        y_ref[:, gs] = y_g


def _ssd_scan(xs, bc, dt, alog, nb, seq, yf=None, dsk=None):
    reverse = yf is not None
    t = xs.shape[0]
    q = SSD_CHUNK
    nc = seq // q
    ee, sel = _ssd_tables(reverse)

    def rblk(bb, j):
        return bb * nc + ((nc - 1 - j) if reverse else j)

    row_spec = pl.BlockSpec((q, SSD_INNER), lambda bb, j: (rblk(bb, j), 0))
    in_specs = [
        row_spec,
        pl.BlockSpec((q, SSD_GN), lambda bb, j: (rblk(bb, j), 0)),
        pl.BlockSpec((q, SSD_GN), lambda bb, j: (rblk(bb, j), 1)),
        pl.BlockSpec((q, LANES), lambda bb, j: (rblk(bb, j), 0)),
        pl.BlockSpec((1, LANES), lambda bb, j: (0, 0)),
        pl.BlockSpec(ee.shape, lambda bb, j: (0, 0)),
        pl.BlockSpec(sel.shape, lambda bb, j: (0, 0)),
    ]
    args = [xs, bc, bc, dt, alog, ee, sel]
    if reverse:
        in_specs += [row_spec, pl.BlockSpec((1, SSD_INNER), lambda bb, j: (0, 0))]
        args += [yf, dsk]
    return pl.pallas_call(
        functools.partial(_ssd_kernel, reverse=reverse),
        grid=(nb, nc),
        in_specs=in_specs,
        out_specs=row_spec,
        out_shape=jax.ShapeDtypeStruct((t, SSD_INNER), F32),
        scratch_shapes=[pltpu.VMEM((SSD_GROUPS, SSD_STATE, SSD_HPG * SSD_HEAD_DIM), F32)],
        compiler_params=_cparams("parallel", "arbitrary"),
        name="ssd_scan_bwd" if reverse else "ssd_scan_fwd",
    )(*args)


def _ple_tail(x, mix, p_ref, wg_ref, wp_ref, gng_ref, png_ref, out_ref):
    x1 = x + mix
    gate = jax.nn.sigmoid(
        jnp.dot(_rms(x1, gng_ref[...]).astype(BF16), wg_ref[...], preferred_element_type=F32))
    e = _rms(jnp.dot(p_ref[...].astype(BF16), wp_ref[...], preferred_element_type=F32), png_ref[...])
    out_ref[...] = x1 + gate * e


def _ssd_post_kernel(y_ref, z_ref, x_ref, p_ref, ng_ref, wo_ref,
                     wg_ref, wp_ref, gng_ref, png_ref, out_ref):
    y = _rms(y_ref[...] * _silu(z_ref[...]), ng_ref[...])
    mix = jnp.dot(y.astype(BF16), wo_ref[...], preferred_element_type=F32)
    _ple_tail(x_ref[...], mix, p_ref, wg_ref, wp_ref, gng_ref, png_ref, out_ref)


def _att_post_kernel(o_ref, gate_ref, x_ref, p_ref, wo_ref, wg_ref, wp_ref, gng_ref, png_ref, out_ref):
    u = o_ref[...] * _silu(gate_ref[...])
    mix = jnp.dot(u.astype(BF16), wo_ref[...], preferred_element_type=F32)
    _ple_tail(x_ref[...], mix, p_ref, wg_ref, wp_ref, gng_ref, png_ref, out_ref)


def _row_spec(tm, width, colblk=0):
    return pl.BlockSpec((tm, width), lambda i: (i, colblk))


def _full_spec(shape):
    return pl.BlockSpec(shape, lambda i: (0,) * len(shape))


def _ssd_post(y, proj, x, p, ng, wo, wg, wp, gng, png):
    t = x.shape[0]
    tm = _tile(t, 512)
    return pl.pallas_call(
        _ssd_post_kernel,
        grid=(t // tm,),
        in_specs=[
            _row_spec(tm, SSD_INNER),
            _row_spec(tm, SSD_INNER, 0),
            _row_spec(tm, D_MODEL), _row_spec(tm, PLE_DIM),
            _full_spec((1, SSD_INNER)),
            _full_spec(wo.shape), _full_spec(wg.shape), _full_spec(wp.shape),
            _full_spec((1, D_MODEL)), _full_spec((1, D_MODEL)),
        ],
        out_specs=_row_spec(tm, D_MODEL),
        out_shape=jax.ShapeDtypeStruct((t, D_MODEL), F32),
        compiler_params=_cparams("parallel"),
        name="ssd_post",
    )(y, proj, x, p, ng, wo, wg, wp, gng, png)


def _att_post(o, proj, x, p, wo, wg, wp, gng, png):
    t = x.shape[0]
    tm = _tile(t, 512)
    return pl.pallas_call(
        _att_post_kernel,
        grid=(t // tm,),
        in_specs=[
            _row_spec(tm, D_MODEL),
            _row_spec(tm, D_MODEL, 3),
            _row_spec(tm, D_MODEL), _row_spec(tm, PLE_DIM),
            _full_spec(wo.shape), _full_spec(wg.shape), _full_spec(wp.shape),
            _full_spec((1, D_MODEL)), _full_spec((1, D_MODEL)),
        ],
        out_specs=_row_spec(tm, D_MODEL),
        out_shape=jax.ShapeDtypeStruct((t, D_MODEL), F32),
        compiler_params=_cparams("parallel"),
        name="att_post",
    )(o, proj, x, p, wo, wg, wp, gng, png)


def _attprep_kernel(q_ref, k_ref, v_ref, qg_ref, kg_ref, qn_ref, kt_ref, vb_ref):
    shape = q_ref.shape
    first = lax.broadcasted_iota(jnp.int32, shape, 1) < ATT_QK_DIM

    def norm(x, g):
        x2 = x * x
        s1 = jnp.sum(jnp.where(first, x2, 0.0), axis=-1, keepdims=True)
        s2 = jnp.sum(jnp.where(first, 0.0, x2), axis=-1, keepdims=True)
        ms = jnp.where(first, s1, s2) * (1.0 / ATT_QK_DIM)
        return x * lax.rsqrt(ms + EPS) * g

    qn_ref[0] = (norm(q_ref[...], qg_ref[...]) * (ATT_QK_DIM ** -0.5)).astype(BF16)
    kt_ref[0] = norm(k_ref[...], kg_ref[...]).T.astype(BF16)
    vb_ref[0] = v_ref[...].astype(BF16)


def _att_prep(proj, qg, kg, nb, seq):
    tl = _tile(seq, 512)
    nl = seq // tl
    nh = ATT_HEADS
    blk = lambda off: pl.BlockSpec((tl, LANES), lambda bb, h, i: (bb * nl + i, off + h))
    return pl.pallas_call(
        _attprep_kernel,
        grid=(nb, nh, nl),
        in_specs=[blk(0), blk(nh), blk(2 * nh), _g3((1, LANES)), _g3((1, LANES))],
        out_specs=[
            pl.BlockSpec((1, tl, LANES), lambda bb, h, i: (bb * nh + h, i, 0)),
            pl.BlockSpec((1, LANES, tl), lambda bb, h, i: (bb * nh + h, 0, i)),
            pl.BlockSpec((1, tl, LANES), lambda bb, h, i: (bb * nh + h, i, 0)),
        ],
        out_shape=[
            jax.ShapeDtypeStruct((nb * nh, seq, LANES), BF16),
            jax.ShapeDtypeStruct((nb * nh, LANES, seq), BF16),
            jax.ShapeDtypeStruct((nb * nh, seq, LANES), BF16),
        ],
        compiler_params=_cparams("parallel", "parallel", "parallel"),
        name="att_prep",
    )(proj, proj, proj, qg, kg)


def _g3(shape):
    return pl.BlockSpec(shape, lambda bb, h, i: (0,) * len(shape))


def _flash_kernel(q_ref, kt_ref, v_ref, slope_ref, lam_ref, sg_ref, o_ref, *, tk, nk, lambda_init):
    tq = q_ref.shape[1]
    q = q_ref[0]
    first = lax.broadcasted_iota(jnp.int32, q.shape, 1) < ATT_QK_DIM
    zero = jnp.zeros_like(q)
    q1 = jnp.where(first, q, zero)
    q2 = jnp.where(first, zero, q)
    slope = slope_ref[0][:, 0:1]
    rel = (lax.broadcasted_iota(jnp.int32, (tq, tk), 1)
           - lax.broadcasted_iota(jnp.int32, (tq, tk), 0))
    q0 = pl.program_id(2) * tq

    def update(qm, kt, vt, bias, m, l, acc):
        s = jnp.dot(qm, kt, preferred_element_type=F32) - bias
        m_new = jnp.maximum(m, jnp.max(s, axis=-1, keepdims=True))
        alpha = jnp.exp(m - m_new)
        p = jnp.exp(s - m_new)
        l = alpha * l + jnp.sum(p, axis=-1, keepdims=True)
        acc = alpha * acc + jnp.dot(p.astype(BF16), vt, preferred_element_type=F32)
        return m_new, l, acc

    def body(kidx, carry):
        m1, l1, a1, m2, l2, a2 = carry
        k0 = pl.multiple_of(kidx * tk, tk)
        kt = kt_ref[0, :, pl.ds(k0, tk)]
        vt = v_ref[0, pl.ds(k0, tk), :]
        bias = jnp.abs(rel + (k0 - q0)).astype(F32) * slope
        m1, l1, a1 = update(q1, kt, vt, bias, m1, l1, a1)
        m2, l2, a2 = update(q2, kt, vt, bias, m2, l2, a2)
        return m1, l1, a1, m2, l2, a2

    neg = jnp.full((tq, 1), -jnp.inf, F32)
    zl = jnp.zeros((tq, 1), F32)
    za = jnp.zeros((tq, ATT_V_DIM), F32)
    m1, l1, a1, m2, l2, a2 = lax.fori_loop(0, nk, body, (neg, zl, za, neg, zl, za))

    lv = lam_ref[...]
    lam = (jnp.exp(jnp.sum(lv[0:1] * lv[1:2], axis=-1, keepdims=True))
           - jnp.exp(jnp.sum(lv[2:3] * lv[3:4], axis=-1, keepdims=True)) + lambda_init)
    o = a1 / l1 - lam * (a2 / l2)
    o_ref[...] = _rms(o, sg_ref[...]) * (1.0 - lambda_init)


def _flash(qn, kt, vb, slopes, lamv, sg, nb, seq, lambda_init):
    nh = ATT_HEADS
    tq = _tile(seq, 512)
    tk = _tile(seq, 512)
    nq = seq // tq
    return pl.pallas_call(
        functools.partial(_flash_kernel, tk=tk, nk=seq // tk, lambda_init=lambda_init),
        grid=(nb, nh, nq),
        in_specs=[
            pl.BlockSpec((1, tq, LANES), lambda bb, h, i: (bb * nh + h, i, 0)),
            pl.BlockSpec((1, LANES, seq), lambda bb, h, i: (bb * nh + h, 0, 0)),
            pl.BlockSpec((1, seq, LANES), lambda bb, h, i: (bb * nh + h, 0, 0)),
            pl.BlockSpec((1, 1, LANES), lambda bb, h, i: (h, 0, 0)),
            _g3((4, ATT_QK_DIM)),
            _g3((1, ATT_V_DIM)),
        ],
        out_specs=pl.BlockSpec((tq, LANES), lambda bb, h, i: (bb * nq + i, h)),
        out_shape=jax.ShapeDtypeStruct((nb * seq, nh * ATT_V_DIM), F32),
        compiler_params=_cparams("parallel", "parallel", "arbitrary"),
        name="flash_diff_attn",
    )(qn, kt, vb, slopes, lamv, sg)


FLASH_TILE = 512
POS_U = 0
POS_W_LOW = 2
POS_OFF = 4
POS_SHIFT = 7
POS_W_UP = 8
BAND_EXP_CUTOFF = 112.0
SHIFT_SAFE_MAX = 40.0


def _split2(x):
    hi = x.astype(BF16).astype(F32)
    lo = (x - hi).astype(BF16).astype(F32)
    return hi, lo


def _split3(x):
    hi = x.astype(BF16).astype(F32)
    r = x - hi
    mid = r.astype(BF16).astype(F32)
    lo = (r - mid).astype(BF16).astype(F32)
    return hi, mid, lo


def _attprep_band_kernel(q_ref, k_ref, v_ref, qg_ref, kg_ref, slope_ref, qn_ref, kt_ref, vb_ref):
    tl = q_ref.shape[0]
    gr = lax.broadcasted_iota(jnp.int32, (3 * LANES, LANES), 0) % LANES
    gc = lax.broadcasted_iota(jnp.int32, (3 * LANES, LANES), 1)
    same_half = jnp.where((gr < ATT_QK_DIM) == (gc < ATT_QK_DIM), 1.0, 0.0).astype(BF16)

    def norm(x, g):
        pieces = jnp.concatenate(_split3(x * x), axis=1).astype(BF16)
        ms = jnp.dot(pieces, same_half, preferred_element_type=F32) * (1.0 / ATT_QK_DIM)
        return x * lax.rsqrt(ms + EPS) * g

    qn_ref[0] = (norm(q_ref[...], qg_ref[...]) * (ATT_QK_DIM ** -0.5)).astype(BF16)
    kt_ref[0, 0:LANES, :] = norm(k_ref[...], kg_ref[...]).T.astype(BF16)

    mq = jnp.max(jnp.abs(qg_ref[...]), axis=-1, keepdims=True)
    mk = jnp.max(jnp.abs(kg_ref[...]), axis=-1, keepdims=True)
    shift = mq * mk * (ATT_QK_DIM ** 0.5 * 1.02)

    slope = slope_ref[0][:, 0:1]
    r = lax.broadcasted_iota(jnp.int32, (LANES, tl), 0)
    j = lax.broadcasted_iota(jnp.int32, (LANES, tl), 1).astype(F32)
    wl_hi, wl_lo = _split2(slope * (tl - 1.0 - j))
    wu_hi, wu_lo = _split2(slope * j)
    minus_one = (r == POS_U) | (r == POS_U + 1) | ((r >= POS_OFF) & (r < POS_OFF + 3))
    pos = jnp.where(minus_one, -1.0, 0.0)
    pos = jnp.where(r == POS_W_LOW, -wl_hi, pos)
    pos = jnp.where(r == POS_W_LOW + 1, -wl_lo, pos)
    pos = jnp.where(r == POS_SHIFT, -shift, pos)
    pos = jnp.where(r == POS_W_UP, -wu_hi, pos)
    pos = jnp.where(r == POS_W_UP + 1, -wu_lo, pos)
    kt_ref[0, LANES:2 * LANES, :] = pos.astype(BF16)

    lane = lax.broadcasted_iota(jnp.int32, (tl, LANES), 1)
    vb_ref[0, :, 0:LANES] = v_ref[...].astype(BF16)
    vb_ref[0, :, LANES:2 * LANES] = jnp.where(lane == 0, 1.0, 0.0).astype(BF16)


def _att_prep_band(proj, qg, kg, slopes, nb, seq):
    tl = _tile(seq, FLASH_TILE)
    nl = seq // tl
    nh = ATT_HEADS
    blk = lambda off: pl.BlockSpec((tl, LANES), lambda bb, h, i: (bb * nl + i, off + h))
    return pl.pallas_call(
        _attprep_band_kernel,
        grid=(nb, nh, nl),
        in_specs=[blk(0), blk(nh), blk(2 * nh), _g3((1, LANES)), _g3((1, LANES)),
                  pl.BlockSpec((1, 1, LANES), lambda bb, h, i: (h, 0, 0))],
        out_specs=[
            pl.BlockSpec((1, tl, LANES), lambda bb, h, i: (bb * nh + h, i, 0)),
            pl.BlockSpec((1, 2 * LANES, tl), lambda bb, h, i: (bb * nh + h, 0, i)),
            pl.BlockSpec((1, tl, 2 * LANES), lambda bb, h, i: (bb * nh + h, i, 0)),
        ],
        out_shape=[
            jax.ShapeDtypeStruct((nb * nh, seq, LANES), BF16),
            jax.ShapeDtypeStruct((nb * nh, 2 * LANES, seq), BF16),
            jax.ShapeDtypeStruct((nb * nh, seq, 2 * LANES), BF16),
        ],
        compiler_params=_cparams("parallel", "parallel", "parallel"),
        name="att_prep_band",
    )(proj, proj, proj, qg, kg, slopes)


def _flash_band_kernel(q_ref, kt_ref, v_ref, slope_ref, lam_ref, sg_ref, o_ref,
                       lhs_ref, acc_ref, p_ref, base_ref, *, tk, nk, lambda_init):
    tq = q_ref.shape[1]
    assert tq == tk
    h = pl.program_id(1)
    qi = pl.program_id(2)
    q0 = qi * tq
    slope_row = slope_ref[0]
    lane = lax.broadcasted_iota(jnp.int32, (tq, LANES), 1)
    rowf = lax.broadcasted_iota(jnp.int32, (tq, LANES), 0).astype(F32)

    q = q_ref[0]
    zero = jnp.zeros_like(q)
    first = lane < ATT_QK_DIM
    lhs_ref[0:tq, 0:LANES] = jnp.where(first, q, zero)
    lhs_ref[tq:2 * tq, 0:LANES] = jnp.where(first, zero, q)
    acc_ref[...] = jnp.zeros(acc_ref.shape, F32)

    def pos_base(u, w_lane):
        u_hi, u_lo = _split2(u)
        ones = (lane == POS_SHIFT) | (lane == w_lane) | (lane == w_lane + 1)
        base = jnp.where(ones, 1.0, 0.0)
        base = jnp.where(lane == POS_U, u_hi, base)
        return jnp.where(lane == POS_U + 1, u_lo, base)

    off_lanes = (lane >= POS_OFF) & (lane < POS_OFF + 3)
    lane1 = lax.broadcasted_iota(jnp.int32, (2 * SUBLANES, LANES), 1)
    slope_tile = jnp.broadcast_to(slope_row, (2 * SUBLANES, LANES))

    def set_pos(base, dist):
        if dist is not None:
            hi, mid, lo = _split3(slope_tile * dist.astype(F32))
            orow = jnp.where(lane1 == POS_OFF, hi, jnp.where(lane1 == POS_OFF + 1, mid, lo))
            base = jnp.where(off_lanes, orow[0:1, :], base)
        pos = base.astype(BF16)
        lhs_ref[0:tq, LANES:2 * LANES] = pos
        lhs_ref[tq:2 * tq, LANES:2 * LANES] = pos

    def scores(kidx, bias, slot):
        k0 = pl.multiple_of(kidx * tk, tk)
        s = jnp.dot(lhs_ref[...], kt_ref[0, :, pl.ds(k0, tk)], preferred_element_type=F32)
        if bias is not None:
            s = s - bias
        p_ref[slot] = jnp.exp(s).astype(BF16)

    def weighted_sum(kidx, slot):
        k0 = pl.multiple_of(kidx * tk, tk)
        acc_ref[...] += jnp.dot(p_ref[slot], v_ref[0, pl.ds(k0, tk), :], preferred_element_type=F32)

    band = lax.shift_left(jnp.int32(int(BAND_EXP_CUTOFF)), h + 1)
    k_lo = jnp.maximum(q0 + 1 - band, 0) // tk
    k_hi = jnp.minimum((q0 + tq - 2 + band + tk) // tk, nk)
    n_low = qi - k_lo
    n_off_diag = n_low + (k_hi - qi - 1)

    base_ref[0] = pos_base(slope_row * rowf, POS_W_LOW)
    base_ref[1] = pos_base(slope_row * (tq - 1.0 - rowf), POS_W_UP)

    set_pos(jnp.where(lane == POS_SHIFT, 1.0, 0.0), None)
    rel = (lax.broadcasted_iota(jnp.int32, (tq, tk), 1)
           - lax.broadcasted_iota(jnp.int32, (tq, tk), 0))
    bias = jnp.abs(rel).astype(F32) * slope_row[:, 0:1]
    scores(qi, jnp.concatenate([bias, bias], axis=0), 0)

    def tile_of(u):
        return jnp.where(u < n_low, k_lo + u, qi + 1 + u - n_low)

    def step(u, prev, slot):
        below = u < n_low
        kidx = tile_of(u)
        dist = jnp.where(below, q0 - kidx * tk - (tk - 1), kidx * tk - q0 - (tq - 1))
        weighted_sum(prev, slot)
        set_pos(base_ref[jnp.where(below, 0, 1)], dist)
        scores(kidx, None, 1 - slot)
        return kidx

    def pair_body(i, prev):
        return step(2 * i + 1, step(2 * i, prev, 0), 1)

    n_pairs = n_off_diag // 2
    prev = lax.fori_loop(0, n_pairs, pair_body, qi)
    odd = n_off_diag - 2 * n_pairs

    @pl.when(odd == 1)
    def _():
        weighted_sum(step(n_off_diag - 1, prev, 0), 1)

    @pl.when(odd == 0)
    def _():
        weighted_sum(prev, 0)

    lv = lam_ref[...]
    lam = (jnp.exp(jnp.sum(lv[0:1] * lv[1:2], axis=-1, keepdims=True))
           - jnp.exp(jnp.sum(lv[2:3] * lv[3:4], axis=-1, keepdims=True)) + lambda_init)
    a1 = acc_ref[0:tq, 0:LANES]
    l1 = acc_ref[0:tq, LANES:LANES + 1]
    a2 = acc_ref[tq:2 * tq, 0:LANES]
    l2 = acc_ref[tq:2 * tq, LANES:LANES + 1]
    o = a1 / l1 - lam * (a2 / l2)
    o_ref[...] = _rms(o, sg_ref[...]) * (1.0 - lambda_init)


def _flash_band(qn, kt, vb, slopes, lamv, sg, nb, seq, lambda_init):
    nh = ATT_HEADS
    tq = _tile(seq, FLASH_TILE)
    nq = seq // tq
    return pl.pallas_call(
        functools.partial(_flash_band_kernel, tk=tq, nk=nq, lambda_init=lambda_init),
        grid=(nb, nh, nq),
        in_specs=[
            pl.BlockSpec((1, tq, LANES), lambda bb, h, i: (bb * nh + h, i, 0)),
            pl.BlockSpec((1, 2 * LANES, seq), lambda bb, h, i: (bb * nh + h, 0, 0)),
            pl.BlockSpec((1, seq, 2 * LANES), lambda bb, h, i: (bb * nh + h, 0, 0)),
            pl.BlockSpec((1, 1, LANES), lambda bb, h, i: (h, 0, 0)),
            _g3((4, ATT_QK_DIM)),
            _g3((1, ATT_V_DIM)),
        ],
        out_specs=pl.BlockSpec((tq, LANES), lambda bb, h, i: (bb * nq + i, h)),
        out_shape=jax.ShapeDtypeStruct((nb * seq, nh * ATT_V_DIM), F32),
        scratch_shapes=[
            pltpu.VMEM((2 * tq, 2 * LANES), BF16),
            pltpu.VMEM((2 * tq, 2 * LANES), F32),
            pltpu.VMEM((2, 2 * tq, tq), BF16),
            pltpu.VMEM((2, tq, LANES), F32),
        ],
        compiler_params=_cparams("parallel", "parallel", "arbitrary"),
        name="flash_band",
    )(qn, kt, vb, slopes, lamv, sg)


def _attention(proj, lw, slopes, nb, seq, lambda_init):
    def banded(proj):
        qn, kt, vb = _att_prep_band(proj, lw["qg"], lw["kg"], slopes, nb, seq)
        return _flash_band(qn, kt, vb, slopes, lw["lam"], lw["sub_g"], nb, seq, lambda_init)

    def online(proj):
        qn, kt, vb = _att_prep(proj, lw["qg"], lw["kg"], nb, seq)
        return _flash(qn, kt, vb, slopes, lw["lam"], lw["sub_g"], nb, seq, lambda_init)

    shift = jnp.max(jnp.abs(lw["qg"])) * jnp.max(jnp.abs(lw["kg"])) * (ATT_QK_DIM ** 0.5 * 1.02)
    return lax.cond(shift < SHIFT_SAFE_MAX, banded, online, proj)


def _alibi_slopes():
    s = np.array([2.0 ** (-8.0 * (i + 1) / ATT_HEADS) for i in range(ATT_HEADS)], dtype=np.float32)
    return jnp.asarray(np.broadcast_to(s[:, None, None], (ATT_HEADS, 1, LANES)).copy())


def _prep_weights(w):
    row = lambda v: v.reshape(1, -1).astype(F32)
    pad_lanes = lambda v: jnp.pad(v.reshape(1, -1).astype(F32), ((0, 0), (0, LANES - v.size)))
    out = {"layers": [], "slopes": _alibi_slopes()}
    for i in range(DEPTH):
        j = i // 2
        lw = {
            "pre_g": row(w["pre_norm_g"][i]),
            "wg": w["ple_w_gate"][i].astype(BF16),
            "wp": w["ple_w_proj"][i].astype(BF16),
            "gng": row(w["ple_gate_norm_g"][i]),
            "png": row(w["ple_norm_g"][i]),
        }
        if i % 2 == 0:
            win = w["ssd_w_in"][j]
            lw.update(
                w_main=win[:, :SSD_MAIN_COLS].astype(BF16),
                w_dt=jnp.pad(win[:, SSD_MAIN_COLS:], ((0, 0), (0, LANES - 2 * SSD_HEADS))).astype(BF16),
                dt_bias=pad_lanes(w["ssd_dt_bias"][j]),
                a_log=pad_lanes(w["ssd_a_log"][j]),
                conv_w=w["ssd_conv_w"][j].astype(F32),
                conv_b=row(w["ssd_conv_b"][j]),
                d_skip=row(jnp.repeat(w["ssd_d_skip"][j], SSD_HEAD_DIM)),
                norm_g=row(w["ssd_norm_g"][j]),
                w_out=w["ssd_w_out"][j].astype(BF16),
            )
        else:
            lw.update(
                w_in=w["att_w_in"][j].astype(BF16),
                qg=row(jnp.tile(w["att_q_norm_g"][j], 2)),
                kg=row(jnp.tile(w["att_k_norm_g"][j], 2)),
                lam=jnp.stack([w["att_lam_q1"][j], w["att_lam_k1"][j],
                               w["att_lam_q2"][j], w["att_lam_k2"][j]]).astype(F32),
                sub_g=row(w["att_sub_norm_g"][j]),
                w_out=w["att_w_out"][j].astype(BF16),
            )
        out["layers"].append(lw)
    return out


def _trunk(x3, p4, pw):
    nb, seq, d = x3.shape
    t = nb * seq
    x = x3.reshape(t, d)
    for i in range(DEPTH):
        lw = pw["layers"][i]
        p = p4[i].reshape(t, PLE_DIM)
        if i % 2 == 0:
            proj, dt = _inproj(x, lw["pre_g"], lw["w_main"], lw["w_dt"], lw["dt_bias"])
            xs = _conv_silu(proj, lw["conv_w"], lw["conv_b"], nb, seq,
                            SSD_INNER, 0, SSD_INNER, F32)
            bc = _conv_silu(proj, lw["conv_w"], lw["conv_b"], nb, seq,
                            2 * SSD_INNER, SSD_INNER, 2 * SSD_GN, BF16)
            yf = _ssd_scan(xs, bc, dt, lw["a_log"], nb, seq)
            y = _ssd_scan(xs, bc, dt, lw["a_log"], nb, seq, yf, lw["d_skip"])
            x = _ssd_post(y, proj, x, p, lw["norm_g"], lw["w_out"],
                          lw["wg"], lw["wp"], lw["gng"], lw["png"])
        else:
            lambda_init = 0.8 - 0.6 * math.exp(-0.3 * i)
            proj = _inproj(x, lw["pre_g"], lw["w_in"])
            o = _attention(proj, lw, pw["slopes"], nb, seq, lambda_init)
            x = _att_post(o, proj, x, p, lw["w_out"], lw["wg"], lw["wp"], lw["gng"], lw["png"])
    return x.reshape(nb, seq, d)


def kernel(x_prompt, x_sample, p_prompt, p_sample, pre_norm_g, ssd_w_in, ssd_conv_w, ssd_conv_b, ssd_dt_bias, ssd_a_log, ssd_d_skip, ssd_norm_g, ssd_w_out, att_w_in, att_q_norm_g, att_k_norm_g, att_lam_q1, att_lam_k1, att_lam_q2, att_lam_k2, att_sub_norm_g, att_w_out, ple_w_proj, ple_norm_g, ple_gate_norm_g, ple_w_gate):
    pw = _prep_weights(dict(
        pre_norm_g=pre_norm_g, ssd_w_in=ssd_w_in, ssd_conv_w=ssd_conv_w, ssd_conv_b=ssd_conv_b,
        ssd_dt_bias=ssd_dt_bias, ssd_a_log=ssd_a_log, ssd_d_skip=ssd_d_skip, ssd_norm_g=ssd_norm_g,
        ssd_w_out=ssd_w_out, att_w_in=att_w_in, att_q_norm_g=att_q_norm_g, att_k_norm_g=att_k_norm_g,
        att_lam_q1=att_lam_q1, att_lam_k1=att_lam_k1, att_lam_q2=att_lam_q2, att_lam_k2=att_lam_k2,
        att_sub_norm_g=att_sub_norm_g, att_w_out=att_w_out, ple_w_proj=ple_w_proj,
        ple_norm_g=ple_norm_g, ple_gate_norm_g=ple_gate_norm_g, ple_w_gate=ple_w_gate))
    return (_trunk(x_prompt, p_prompt, pw), _trunk(x_sample, p_sample, pw))
```

```python
import functools
import math

import numpy as np
import jax
import jax.numpy as jnp
from jax import lax
from jax.experimental import pallas as pl
from jax.experimental.pallas import tpu as pltpu

F32 = jnp.float32
BF16 = jnp.bfloat16

EPS = 1e-6
D_MODEL = 1024
DEPTH = 4
PLE_DIM = 256

SSD_INNER = 2048
SSD_HEAD_DIM = 64
SSD_HEADS = 32
SSD_GROUPS = 8
SSD_HPG = 4
SSD_STATE = 128
SSD_GN = SSD_GROUPS * SSD_STATE
SSD_CONV_W = 5
SSD_CHUNK = 128
SSD_CHUNKS_PER_STEP = 4
SSD_MAIN_COLS = SSD_INNER + SSD_INNER + 2 * SSD_GN

ATT_HEADS = 8
ATT_QK_DIM = 64
ATT_V_DIM = 128
ATT_IN_COLS = 4096

LANES = 128
SUBLANES = 8
VMEM_LIMIT_BYTES = 56 * 1024 * 1024


def _cparams(*sem):
    return pltpu.CompilerParams(dimension_semantics=sem, vmem_limit_bytes=VMEM_LIMIT_BYTES)


def _tile(n, pref):
    t = min(n, pref)
    assert n % t == 0, (n, pref)
    return t


def _rms(x, g):
    return x * lax.rsqrt(jnp.mean(x * x, axis=-1, keepdims=True) + EPS) * g


def _silu(x):
    return x * jax.nn.sigmoid(x)


def _inproj_kernel(x_ref, g_ref, w_ref, *rest, has_dt):
    if has_dt:
        wdt_ref, dtb_ref, out_ref, dt_ref, hn_ref = rest
    else:
        out_ref, hn_ref = rest

    @pl.when(pl.program_id(1) == 0)
    def _():
        hn_ref[...] = _rms(x_ref[...], g_ref[...]).astype(BF16)
        if has_dt:
            raw = jnp.dot(hn_ref[...], wdt_ref[...], preferred_element_type=F32) + dtb_ref[...]
            dt_ref[...] = jax.nn.softplus(raw)

    out_ref[...] = jnp.dot(hn_ref[...], w_ref[...], preferred_element_type=F32)


def _inproj(x, g, w, wdt=None, dtb=None):
    t, d = x.shape
    n = w.shape[1]
    tm = _tile(t, 1024)
    tn = _tile(n, 2048)
    has_dt = wdt is not None
    in_specs = [
        pl.BlockSpec((tm, d), lambda i, j: (i, 0)),
        pl.BlockSpec((1, d), lambda i, j: (0, 0)),
        pl.BlockSpec((d, tn), lambda i, j: (0, j)),
    ]
    args = [x, g, w]
    out_shape = [jax.ShapeDtypeStruct((t, n), F32)]
    out_specs = [pl.BlockSpec((tm, tn), lambda i, j: (i, j))]
    if has_dt:
        in_specs += [pl.BlockSpec((d, LANES), lambda i, j: (0, 0)),
                     pl.BlockSpec((1, LANES), lambda i, j: (0, 0))]
        args += [wdt, dtb]
        out_shape.append(jax.ShapeDtypeStruct((t, LANES), F32))
        out_specs.append(pl.BlockSpec((tm, LANES), lambda i, j: (i, 0)))
    res = pl.pallas_call(
        functools.partial(_inproj_kernel, has_dt=has_dt),
        grid=(t // tm, n // tn),
        in_specs=in_specs,
        out_specs=out_specs,
        out_shape=out_shape,
        scratch_shapes=[pltpu.VMEM((tm, d), BF16)],
        compiler_params=_cparams("parallel", "arbitrary"),
        name="inproj_dt" if has_dt else "inproj",
    )(*args)
    return res if has_dt else res[0]


def _conv_kernel(prev_ref, cur_ref, next_ref, w_ref, b_ref, out_ref, *, nl):
    i = pl.program_id(1)
    tl = cur_ref.shape[0]
    prev = jnp.where(i > 0, prev_ref[...], 0.0)
    nxt = jnp.where(i < nl - 1, next_ref[...], 0.0)
    ext = jnp.concatenate([prev, cur_ref[...], nxt], axis=0)
    n = tl + 2 * SUBLANES
    acc = jnp.zeros(cur_ref.shape, F32) + b_ref[...]
    half = SSD_CONV_W // 2
    for k in range(SSD_CONV_W):
        shift = (half - k) % n
        r = ext if shift == 0 else pltpu.roll(ext, shift, 0)
        acc = acc + r[SUBLANES:SUBLANES + tl] * w_ref[k:k + 1, :]
    out_ref[...] = _silu(acc).astype(out_ref.dtype)


def _conv_silu(proj, w, b, nb, seq, col0_in, col0_w, ncols, out_dtype):
    t = proj.shape[0]
    cb = 512
    tl = _tile(seq, 512)
    nl = seq // tl
    rb = tl // SUBLANES
    nrb = t // SUBLANES
    ci, cw = col0_in // cb, col0_w // cb
    return pl.pallas_call(
        functools.partial(_conv_kernel, nl=nl),
        grid=(nb, nl, ncols // cb),
        in_specs=[
            pl.BlockSpec((SUBLANES, cb), lambda bb, i, c: (jnp.maximum((bb * nl + i) * rb - 1, 0), ci + c)),
            pl.BlockSpec((tl, cb), lambda bb, i, c: (bb * nl + i, ci + c)),
            pl.BlockSpec((SUBLANES, cb), lambda bb, i, c: (jnp.minimum((bb * nl + i + 1) * rb, nrb - 1), ci + c)),
            pl.BlockSpec((SSD_CONV_W, cb), lambda bb, i, c: (0, cw + c)),
            pl.BlockSpec((1, cb), lambda bb, i, c: (0, cw + c)),
        ],
        out_specs=pl.BlockSpec((tl, cb), lambda bb, i, c: (bb * nl + i, c)),
        out_shape=jax.ShapeDtypeStruct((t, ncols), out_dtype),
        compiler_params=_cparams("parallel", "parallel", "parallel"),
        name="conv_silu",
    )(proj, proj, proj, w, b)


def _ssd_tables(reverse):
    lane0 = SSD_HEADS if reverse else 0
    ee = np.zeros((2 * LANES, SSD_INNER), np.float32)
    sel = np.zeros((3 * LANES, SSD_HEADS * LANES), np.float32)
    for h in range(SSD_HEADS):
        for piece in range(2):
            ee[piece * LANES + lane0 + h, h * SSD_HEAD_DIM:(h + 1) * SSD_HEAD_DIM] = 1.0
        for piece in range(3):
            sel[piece * LANES + lane0 + h, h * LANES:(h + 1) * LANES] = 1.0
    return jnp.asarray(ee, BF16), jnp.asarray(sel, BF16)


def _ssd_kernel(xs_ref, b_ref, c_ref, dt_ref, alog_ref, ee_ref, sel_ref, *rest, reverse):
    q = SSD_CHUNK
    gw = SSD_HPG * SSD_HEAD_DIM
    if reverse:
        yf_ref, dsk_ref, y_ref, state_ref = rest
    else:
        y_ref, state_ref = rest

    @pl.when(pl.program_id(1) == 0)
    def _():
        state_ref[...] = jnp.zeros(state_ref.shape, F32)

    lane0 = SSD_HEADS if reverse else 0
    row = lax.broadcasted_iota(jnp.int32, (q, q), 0)
    col = lax.broadcasted_iota(jnp.int32, (q, q), 1)
    mask = (row <= col) if reverse else (row >= col)
    tri = jnp.where(mask, 1.0, 0.0).astype(F32)
    head_of_lane = lax.broadcasted_iota(jnp.int32, (q, gw), 1) // SSD_HEAD_DIM

    pieces2 = lambda x: jnp.concatenate(_split2(x), axis=1).astype(BF16)

    def chunk(rows):
        dtc = dt_ref[rows]
        a = dtc * (-jnp.exp(alog_ref[...]))
        c = jnp.dot(tri, a, precision=lax.Precision.HIGHEST, preferred_element_type=F32)
        c_t = c.T
        tot = c[0:1, :] if reverse else c[q - 1:q, :]
        ec = jnp.exp(c)
        ed = jnp.exp(tot - c)

        lhs_w = jnp.concatenate([pieces2(dtc), pieces2(dtc * ed), pieces2(ec)], axis=0)
        spread = jnp.dot(lhs_w, ee_ref[...], preferred_element_type=F32)
        c3 = jnp.concatenate(_split3(c), axis=1).astype(BF16)
        c_cols = jnp.dot(c3, sel_ref[...], preferred_element_type=F32)

        for g in range(SSD_GROUPS):
            gs = slice(g * gw, (g + 1) * gw)
            bg = b_ref[rows, g * SSD_STATE:(g + 1) * SSD_STATE]
            cg = c_ref[rows, g * SSD_STATE:(g + 1) * SSD_STATE]
            dt_e, dted_e, ec_e = spread[0:q, gs], spread[q:2 * q, gs], spread[2 * q:3 * q, gs]
            xs_g = xs_ref[rows, gs]
            xdt = (xs_g * dt_e).astype(BF16)
            xw = (xs_g * dted_e).astype(BF16)
            cb = lax.dot_general(cg, bg, (((1,), (1,)), ((), ())), preferred_element_type=F32)
            bg_t = bg.astype(F32).T.astype(BF16)
            st_new = jnp.dot(bg_t, xw, preferred_element_type=F32)
            prev = state_ref[g]
            y_off = jnp.dot(cg, prev.astype(BF16), preferred_element_type=F32) * ec_e
            etot_e = ec_e[0:1, :] if reverse else ec_e[q - 1:q, :]
            state_ref[g] = prev * etot_e + st_new
            ms, xd = [], []
            for r in range(SSD_HPG):
                h = g * SSD_HPG + r
                seg = c_cols[:, h * LANES:(h + 1) * LANES] - c_t[lane0 + h:lane0 + h + 1, :]
                lmat = jnp.exp(jnp.where(mask, seg, -jnp.inf))
                ms.append((cb * lmat).astype(BF16))
                xd.append(jnp.where(head_of_lane == r, xdt, jnp.zeros_like(xdt)))
            y_diag = jnp.dot(jnp.concatenate(ms, axis=1), jnp.concatenate(xd, axis=0),
                             preferred_element_type=F32)
            y_g = y_diag + y_off
            if reverse:
                y_g = yf_ref[rows, gs] + y_g + xs_g * dsk_ref[:, gs]
            y_ref[rows, gs] = y_g

    n_chunks = dt_ref.shape[0] // q
    for ci in (reversed(range(n_chunks)) if reverse else range(n_chunks)):
        chunk(slice(ci * q, (ci + 1) * q))


def _ssd_scan(xs, bc, dt, alog, nb, seq, yf=None, dsk=None):
    reverse = yf is not None
    t = xs.shape[0]
    q = _tile(seq, SSD_CHUNKS_PER_STEP * SSD_CHUNK)
    nc = seq // q
    ee, sel = _ssd_tables(reverse)

    def rblk(bb, j):
        return bb * nc + ((nc - 1 - j) if reverse else j)

    row_spec = pl.BlockSpec((q, SSD_INNER), lambda bb, j: (rblk(bb, j), 0))
    in_specs = [
        row_spec,
        pl.BlockSpec((q, SSD_GN), lambda bb, j: (rblk(bb, j), 0)),
        pl.BlockSpec((q, SSD_GN), lambda bb, j: (rblk(bb, j), 1)),
        pl.BlockSpec((q, LANES), lambda bb, j: (rblk(bb, j), 0)),
        pl.BlockSpec((1, LANES), lambda bb, j: (0, 0)),
        pl.BlockSpec(ee.shape, lambda bb, j: (0, 0)),
        pl.BlockSpec(sel.shape, lambda bb, j: (0, 0)),
    ]
    args = [xs, bc, bc, dt, alog, ee, sel]
    if reverse:
        in_specs += [row_spec, pl.BlockSpec((1, SSD_INNER), lambda bb, j: (0, 0))]
        args += [yf, dsk]
    return pl.pallas_call(
        functools.partial(_ssd_kernel, reverse=reverse),
        grid=(nb, nc),
        in_specs=in_specs,
        out_specs=row_spec,
        out_shape=jax.ShapeDtypeStruct((t, SSD_INNER), F32),
        scratch_shapes=[pltpu.VMEM((SSD_GROUPS, SSD_STATE, SSD_HPG * SSD_HEAD_DIM), F32)],
        compiler_params=_cparams("parallel", "arbitrary"),
        name="ssd_scan_bwd" if reverse else "ssd_scan_fwd",
    )(*args)


def _ple_tail(x, mix, p_ref, wg_ref, wp_ref, gng_ref, png_ref, out_ref):
    x1 = x + mix
    gate = jax.nn.sigmoid(
        jnp.dot(_rms(x1, gng_ref[...]).astype(BF16), wg_ref[...], preferred_element_type=F32))
    e = _rms(jnp.dot(p_ref[...].astype(BF16), wp_ref[...], preferred_element_type=F32), png_ref[...])
    out_ref[...] = x1 + gate * e


def _ssd_post_kernel(y_ref, z_ref, x_ref, p_ref, ng_ref, wo_ref,
                     wg_ref, wp_ref, gng_ref, png_ref, out_ref):
    y = _rms(y_ref[...] * _silu(z_ref[...]), ng_ref[...])
    mix = jnp.dot(y.astype(BF16), wo_ref[...], preferred_element_type=F32)
    _ple_tail(x_ref[...], mix, p_ref, wg_ref, wp_ref, gng_ref, png_ref, out_ref)


def _att_post_kernel(o_ref, gate_ref, x_ref, p_ref, wo_ref, wg_ref, wp_ref, gng_ref, png_ref, out_ref):
    u = o_ref[...] * _silu(gate_ref[...])
    mix = jnp.dot(u.astype(BF16), wo_ref[...], preferred_element_type=F32)
    _ple_tail(x_ref[...], mix, p_ref, wg_ref, wp_ref, gng_ref, png_ref, out_ref)


def _row_spec(tm, width, colblk=0):
    return pl.BlockSpec((tm, width), lambda i: (i, colblk))


def _full_spec(shape):
    return pl.BlockSpec(shape, lambda i: (0,) * len(shape))


def _ssd_post(y, proj, x, p, ng, wo, wg, wp, gng, png):
    t = x.shape[0]
    tm = _tile(t, 512)
    return pl.pallas_call(
        _ssd_post_kernel,
        grid=(t // tm,),
        in_specs=[
            _row_spec(tm, SSD_INNER),
            _row_spec(tm, SSD_INNER, 0),
            _row_spec(tm, D_MODEL), _row_spec(tm, PLE_DIM),
            _full_spec((1, SSD_INNER)),
            _full_spec(wo.shape), _full_spec(wg.shape), _full_spec(wp.shape),
            _full_spec((1, D_MODEL)), _full_spec((1, D_MODEL)),
        ],
        out_specs=_row_spec(tm, D_MODEL),
        out_shape=jax.ShapeDtypeStruct((t, D_MODEL), F32),
        compiler_params=_cparams("parallel"),
        name="ssd_post",
    )(y, proj, x, p, ng, wo, wg, wp, gng, png)


def _att_post(o, proj, x, p, wo, wg, wp, gng, png):
    t = x.shape[0]
    tm = _tile(t, 512)
    return pl.pallas_call(
        _att_post_kernel,
        grid=(t // tm,),
        in_specs=[
            _row_spec(tm, D_MODEL),
            _row_spec(tm, D_MODEL, 3),
            _row_spec(tm, D_MODEL), _row_spec(tm, PLE_DIM),
            _full_spec(wo.shape), _full_spec(wg.shape), _full_spec(wp.shape),
            _full_spec((1, D_MODEL)), _full_spec((1, D_MODEL)),
        ],
        out_specs=_row_spec(tm, D_MODEL),
        out_shape=jax.ShapeDtypeStruct((t, D_MODEL), F32),
        compiler_params=_cparams("parallel"),
        name="att_post",
    )(o, proj, x, p, wo, wg, wp, gng, png)


def _attprep_kernel(q_ref, k_ref, v_ref, qg_ref, kg_ref, qn_ref, kt_ref, vb_ref):
    shape = q_ref.shape
    first = lax.broadcasted_iota(jnp.int32, shape, 1) < ATT_QK_DIM

    def norm(x, g):
        x2 = x * x
        s1 = jnp.sum(jnp.where(first, x2, 0.0), axis=-1, keepdims=True)
        s2 = jnp.sum(jnp.where(first, 0.0, x2), axis=-1, keepdims=True)
        ms = jnp.where(first, s1, s2) * (1.0 / ATT_QK_DIM)
        return x * lax.rsqrt(ms + EPS) * g

    qn_ref[0] = (norm(q_ref[...], qg_ref[...]) * (ATT_QK_DIM ** -0.5)).astype(BF16)
    kt_ref[0] = norm(k_ref[...], kg_ref[...]).T.astype(BF16)
    vb_ref[0] = v_ref[...].astype(BF16)


def _att_prep(proj, qg, kg, nb, seq):
    tl = _tile(seq, 512)
    nl = seq // tl
    nh = ATT_HEADS
    blk = lambda off: pl.BlockSpec((tl, LANES), lambda bb, h, i: (bb * nl + i, off + h))
    return pl.pallas_call(
        _attprep_kernel,
        grid=(nb, nh, nl),
        in_specs=[blk(0), blk(nh), blk(2 * nh), _g3((1, LANES)), _g3((1, LANES))],
        out_specs=[
            pl.BlockSpec((1, tl, LANES), lambda bb, h, i: (bb * nh + h, i, 0)),
            pl.BlockSpec((1, LANES, tl), lambda bb, h, i: (bb * nh + h, 0, i)),
            pl.BlockSpec((1, tl, LANES), lambda bb, h, i: (bb * nh + h, i, 0)),
        ],
        out_shape=[
            jax.ShapeDtypeStruct((nb * nh, seq, LANES), BF16),
            jax.ShapeDtypeStruct((nb * nh, LANES, seq), BF16),
            jax.ShapeDtypeStruct((nb * nh, seq, LANES), BF16),
        ],
        compiler_params=_cparams("parallel", "parallel", "parallel"),
        name="att_prep",
    )(proj, proj, proj, qg, kg)


def _g3(shape):
    return pl.BlockSpec(shape, lambda bb, h, i: (0,) * len(shape))


def _flash_kernel(q_ref, kt_ref, v_ref, slope_ref, lam_ref, sg_ref, o_ref, *, tk, nk, lambda_init):
    tq = q_ref.shape[1]
    q = q_ref[0]
    first = lax.broadcasted_iota(jnp.int32, q.shape, 1) < ATT_QK_DIM
    zero = jnp.zeros_like(q)
    q1 = jnp.where(first, q, zero)
    q2 = jnp.where(first, zero, q)
    slope = slope_ref[0][:, 0:1]
    rel = (lax.broadcasted_iota(jnp.int32, (tq, tk), 1)
           - lax.broadcasted_iota(jnp.int32, (tq, tk), 0))
    q0 = pl.program_id(2) * tq

    def update(qm, kt, vt, bias, m, l, acc):
        s = jnp.dot(qm, kt, preferred_element_type=F32) - bias
        m_new = jnp.maximum(m, jnp.max(s, axis=-1, keepdims=True))
        alpha = jnp.exp(m - m_new)
        p = jnp.exp(s - m_new)
        l = alpha * l + jnp.sum(p, axis=-1, keepdims=True)
        acc = alpha * acc + jnp.dot(p.astype(BF16), vt, preferred_element_type=F32)
        return m_new, l, acc

    def body(kidx, carry):
        m1, l1, a1, m2, l2, a2 = carry
        k0 = pl.multiple_of(kidx * tk, tk)
        kt = kt_ref[0, :, pl.ds(k0, tk)]
        vt = v_ref[0, pl.ds(k0, tk), :]
        bias = jnp.abs(rel + (k0 - q0)).astype(F32) * slope
        m1, l1, a1 = update(q1, kt, vt, bias, m1, l1, a1)
        m2, l2, a2 = update(q2, kt, vt, bias, m2, l2, a2)
        return m1, l1, a1, m2, l2, a2

    neg = jnp.full((tq, 1), -jnp.inf, F32)
    zl = jnp.zeros((tq, 1), F32)
    za = jnp.zeros((tq, ATT_V_DIM), F32)
    m1, l1, a1, m2, l2, a2 = lax.fori_loop(0, nk, body, (neg, zl, za, neg, zl, za))

    lv = lam_ref[...]
    lam = (jnp.exp(jnp.sum(lv[0:1] * lv[1:2], axis=-1, keepdims=True))
           - jnp.exp(jnp.sum(lv[2:3] * lv[3:4], axis=-1, keepdims=True)) + lambda_init)
    o = a1 / l1 - lam * (a2 / l2)
    o_ref[...] = _rms(o, sg_ref[...]) * (1.0 - lambda_init)


def _flash(qn, kt, vb, slopes, lamv, sg, nb, seq, lambda_init):
    nh = ATT_HEADS
    tq = _tile(seq, 512)
    tk = _tile(seq, 512)
    nq = seq // tq
    return pl.pallas_call(
        functools.partial(_flash_kernel, tk=tk, nk=seq // tk, lambda_init=lambda_init),
        grid=(nb, nh, nq),
        in_specs=[
            pl.BlockSpec((1, tq, LANES), lambda bb, h, i: (bb * nh + h, i, 0)),
            pl.BlockSpec((1, LANES, seq), lambda bb, h, i: (bb * nh + h, 0, 0)),
            pl.BlockSpec((1, seq, LANES), lambda bb, h, i: (bb * nh + h, 0, 0)),
            pl.BlockSpec((1, 1, LANES), lambda bb, h, i: (h, 0, 0)),
            _g3((4, ATT_QK_DIM)),
            _g3((1, ATT_V_DIM)),
        ],
        out_specs=pl.BlockSpec((tq, LANES), lambda bb, h, i: (bb * nq + i, h)),
        out_shape=jax.ShapeDtypeStruct((nb * seq, nh * ATT_V_DIM), F32),
        compiler_params=_cparams("parallel", "parallel", "arbitrary"),
        name="flash_diff_attn",
    )(qn, kt, vb, slopes, lamv, sg)


FLASH_TILE = 512
POS_U = 0
POS_W_LOW = 2
POS_OFF = 4
POS_SHIFT = 7
POS_W_UP = 8
BAND_EXP_CUTOFF = 112.0
SHIFT_SAFE_MAX = 40.0


def _split2(x):
    hi = x.astype(BF16).astype(F32)
    lo = (x - hi).astype(BF16).astype(F32)
    return hi, lo


def _split3(x):
    hi = x.astype(BF16).astype(F32)
    r = x - hi
    mid = r.astype(BF16).astype(F32)
    lo = (r - mid).astype(BF16).astype(F32)
    return hi, mid, lo


def _attprep_band_kernel(q_ref, k_ref, v_ref, qg_ref, kg_ref, slope_ref, qn_ref, kt_ref, vb_ref):
    tl = q_ref.shape[0]
    gr = lax.broadcasted_iota(jnp.int32, (3 * LANES, LANES), 0) % LANES
    gc = lax.broadcasted_iota(jnp.int32, (3 * LANES, LANES), 1)
    same_half = jnp.where((gr < ATT_QK_DIM) == (gc < ATT_QK_DIM), 1.0, 0.0).astype(BF16)

    def norm(x, g):
        pieces = jnp.concatenate(_split3(x * x), axis=1).astype(BF16)
        ms = jnp.dot(pieces, same_half, preferred_element_type=F32) * (1.0 / ATT_QK_DIM)
        return x * lax.rsqrt(ms + EPS) * g

    mq = jnp.max(jnp.abs(qg_ref[...]), axis=-1, keepdims=True)
    mk = jnp.max(jnp.abs(kg_ref[...]), axis=-1, keepdims=True)
    shift = mq * mk * (ATT_QK_DIM ** 0.5 * 1.02)

    r = lax.broadcasted_iota(jnp.int32, (LANES, tl), 0)
    j = lax.broadcasted_iota(jnp.int32, (LANES, tl), 1).astype(F32)
    minus_one = (r == POS_U) | (r == POS_U + 1) | ((r >= POS_OFF) & (r < POS_OFF + 3))
    pos_const = jnp.where(r == POS_SHIFT, -shift, jnp.where(minus_one, -1.0, 0.0))
    ones = jnp.ones((tl, LANES), BF16)

    for h in range(ATT_HEADS):
        hs = slice(h * LANES, (h + 1) * LANES)
        qn_ref[h] = (norm(q_ref[:, hs], qg_ref[...]) * (ATT_QK_DIM ** -0.5)).astype(BF16)
        kt_ref[h, 0:LANES, :] = norm(k_ref[:, hs], kg_ref[...]).T.astype(BF16)
        slope = slope_ref[h][:, 0:1]
        wl_hi, wl_lo = _split2(slope * (tl - 1.0 - j))
        wu_hi, wu_lo = _split2(slope * j)
        pos = jnp.where(r == POS_W_LOW, -wl_hi, pos_const)
        pos = jnp.where(r == POS_W_LOW + 1, -wl_lo, pos)
        pos = jnp.where(r == POS_W_UP, -wu_hi, pos)
        pos = jnp.where(r == POS_W_UP + 1, -wu_lo, pos)
        kt_ref[h, LANES:2 * LANES, :] = pos.astype(BF16)
        vb_ref[h, :, 0:LANES] = v_ref[:, hs].astype(BF16)
        vb_ref[h, :, LANES:2 * LANES] = ones


def _att_prep_band(proj, qg, kg, slopes, nb, seq):
    tl = _tile(seq, FLASH_TILE)
    nl = seq // tl
    nh = ATT_HEADS
    w = nh * LANES
    blk = lambda c: pl.BlockSpec((tl, w), lambda bb, i: (bb * nl + i, c))
    full = lambda shape: pl.BlockSpec(shape, lambda bb, i: (0,) * len(shape))
    return pl.pallas_call(
        _attprep_band_kernel,
        grid=(nb, nl),
        in_specs=[blk(0), blk(1), blk(2), full((1, LANES)), full((1, LANES)),
                  full((nh, 1, LANES))],
        out_specs=[
            pl.BlockSpec((nh, tl, LANES), lambda bb, i: (bb, i, 0)),
            pl.BlockSpec((nh, 2 * LANES, tl), lambda bb, i: (bb, 0, i)),
            pl.BlockSpec((nh, tl, 2 * LANES), lambda bb, i: (bb, i, 0)),
        ],
        out_shape=[
            jax.ShapeDtypeStruct((nb * nh, seq, LANES), BF16),
            jax.ShapeDtypeStruct((nb * nh, 2 * LANES, seq), BF16),
            jax.ShapeDtypeStruct((nb * nh, seq, 2 * LANES), BF16),
        ],
        compiler_params=_cparams("parallel", "parallel"),
        name="att_prep_band",
    )(proj, proj, proj, qg, kg, slopes)


def _flash_band_kernel(q_ref, kt_ref, v_ref, slope_ref, lam_ref, sg_ref, o_ref,
                       lhs_ref, acc_ref, p_ref, base_ref, bias_ref, *, tk, nk, lambda_init):
    tq = q_ref.shape[1]
    assert tq == tk
    h = pl.program_id(1)
    qi = pl.program_id(2)
    q0 = qi * tq
    slope_row = slope_ref[0]
    lane = lax.broadcasted_iota(jnp.int32, (tq, LANES), 1)
    rowf = lax.broadcasted_iota(jnp.int32, (tq, LANES), 0).astype(F32)

    q = q_ref[0]
    zero = jnp.zeros_like(q)
    first = lane < ATT_QK_DIM
    lhs_ref[0:tq, 0:LANES] = jnp.where(first, q, zero)
    lhs_ref[tq:2 * tq, 0:LANES] = jnp.where(first, zero, q)
    acc_ref[...] = jnp.zeros(acc_ref.shape, F32)

    def pos_base(u, w_lane):
        u_hi, u_lo = _split2(u)
        ones = (lane == POS_SHIFT) | (lane == w_lane) | (lane == w_lane + 1)
        base = jnp.where(ones, 1.0, 0.0)
        base = jnp.where(lane == POS_U, u_hi, base)
        return jnp.where(lane == POS_U + 1, u_lo, base)

    off_lanes = (lane >= POS_OFF) & (lane < POS_OFF + 3)
    lane1 = lax.broadcasted_iota(jnp.int32, (2 * SUBLANES, LANES), 1)
    slope_tile = jnp.broadcast_to(slope_row, (2 * SUBLANES, LANES))

    def set_pos(base, dist):
        if dist is not None:
            hi, mid, lo = _split3(slope_tile * dist.astype(F32))
            orow = jnp.where(lane1 == POS_OFF, hi, jnp.where(lane1 == POS_OFF + 1, mid, lo))
            base = jnp.where(off_lanes, orow[0:1, :], base)
        pos = base.astype(BF16)
        lhs_ref[0:tq, LANES:2 * LANES] = pos
        lhs_ref[tq:2 * tq, LANES:2 * LANES] = pos

    halves = (slice(0, tq), slice(tq, 2 * tq))

    def scores(kidx, slot):
        k0 = pl.multiple_of(kidx * tk, tk)
        s = jnp.dot(lhs_ref[...], kt_ref[0, :, pl.ds(k0, tk)], preferred_element_type=F32)
        p_ref[slot] = jnp.exp(s).astype(BF16)

    def weighted_sum(kidx, slot):
        k0 = pl.multiple_of(kidx * tk, tk)
        acc_ref[...] += jnp.dot(p_ref[slot], v_ref[0, pl.ds(k0, tk), :], preferred_element_type=F32)


    band = lax.shift_left(jnp.int32(int(BAND_EXP_CUTOFF)), h + 1)
    k_lo = jnp.maximum(q0 + 1 - band, 0) // tk
    k_hi = jnp.minimum((q0 + tq - 2 + band + tk) // tk, nk)
    n_low = qi - k_lo
    n_off_diag = n_low + (k_hi - qi - 1)

    @pl.when(qi == 0)
    def _():
        base_ref[0] = pos_base(slope_row * rowf, POS_W_LOW)
        base_ref[1] = pos_base(slope_row * (tq - 1.0 - rowf), POS_W_UP)
        rel = (lax.broadcasted_iota(jnp.int32, (tq, tk), 1)
               - lax.broadcasted_iota(jnp.int32, (tq, tk), 0))
        bias_ref[...] = jnp.abs(rel).astype(F32) * slope_row[:, 0:1]

    set_pos(jnp.where(lane == POS_SHIFT, 1.0, 0.0), None)
    s = jnp.dot(lhs_ref[...], kt_ref[0, :, pl.ds(pl.multiple_of(q0, tk), tk)],
                preferred_element_type=F32)
    for rows in halves:
        p_ref[0, rows] = jnp.exp(s[rows] - bias_ref[...]).astype(BF16)

    def tile_of(u):
        return jnp.where(u < n_low, k_lo + u, qi + 1 + u - n_low)

    def step(u, prev, slot):
        below = u < n_low
        kidx = tile_of(u)
        dist = jnp.where(below, q0 - kidx * tk - (tk - 1), kidx * tk - q0 - (tq - 1))
        weighted_sum(prev, slot)
        set_pos(base_ref[jnp.where(below, 0, 1)], dist)
        scores(kidx, 1 - slot)
        return kidx

    def pair_body(i, prev):
        return step(2 * i + 1, step(2 * i, prev, 0), 1)

    n_pairs = n_off_diag // 2
    prev = lax.fori_loop(0, n_pairs, pair_body, qi)
    odd = n_off_diag - 2 * n_pairs

    @pl.when(odd == 1)
    def _():
        weighted_sum(step(n_off_diag - 1, prev, 0), 1)

    @pl.when(odd == 0)
    def _():
        weighted_sum(prev, 0)

    lv = lam_ref[...]
    lam = (jnp.exp(jnp.sum(lv[0:1] * lv[1:2], axis=-1, keepdims=True))
           - jnp.exp(jnp.sum(lv[2:3] * lv[3:4], axis=-1, keepdims=True)) + lambda_init)
    a1 = acc_ref[0:tq, 0:LANES]
    l1 = acc_ref[0:tq, LANES:2 * LANES]
    a2 = acc_ref[tq:2 * tq, 0:LANES]
    l2 = acc_ref[tq:2 * tq, LANES:2 * LANES]
    o = a1 / l1 - lam * (a2 / l2)
    o_ref[...] = _rms(o, sg_ref[...]) * (1.0 - lambda_init)


def _flash_band(qn, kt, vb, slopes, lamv, sg, nb, seq, lambda_init):
    nh = ATT_HEADS
    tq = _tile(seq, FLASH_TILE)
    nq = seq // tq
    return pl.pallas_call(
        functools.partial(_flash_band_kernel, tk=tq, nk=nq, lambda_init=lambda_init),
        grid=(nb, nh, nq),
        in_specs=[
            pl.BlockSpec((1, tq, LANES), lambda bb, h, i: (bb * nh + h, i, 0)),
            pl.BlockSpec((1, 2 * LANES, seq), lambda bb, h, i: (bb * nh + h, 0, 0)),
            pl.BlockSpec((1, seq, 2 * LANES), lambda bb, h, i: (bb * nh + h, 0, 0)),
            pl.BlockSpec((1, 1, LANES), lambda bb, h, i: (h, 0, 0)),
            _g3((4, ATT_QK_DIM)),
            _g3((1, ATT_V_DIM)),
        ],
        out_specs=pl.BlockSpec((tq, LANES), lambda bb, h, i: (bb * nq + i, h)),
        out_shape=jax.ShapeDtypeStruct((nb * seq, nh * ATT_V_DIM), F32),
        scratch_shapes=[
            pltpu.VMEM((2 * tq, 2 * LANES), BF16),
            pltpu.VMEM((2 * tq, 2 * LANES), F32),
            pltpu.VMEM((2, 2 * tq, tq), BF16),
            pltpu.VMEM((2, tq, LANES), F32),
            pltpu.VMEM((tq, tq), F32),
        ],
        compiler_params=_cparams("parallel", "parallel", "arbitrary"),
        name="flash_band",
    )(qn, kt, vb, slopes, lamv, sg)


def _attention(proj, lw, slopes, nb, seq, lambda_init):
    def banded(proj):
        qn, kt, vb = _att_prep_band(proj, lw["qg"], lw["kg"], slopes, nb, seq)
        return _flash_band(qn, kt, vb, slopes, lw["lam"], lw["sub_g"], nb, seq, lambda_init)

    def online(proj):
        qn, kt, vb = _att_prep(proj, lw["qg"], lw["kg"], nb, seq)
        return _flash(qn, kt, vb, slopes, lw["lam"], lw["sub_g"], nb, seq, lambda_init)

    shift = jnp.max(jnp.abs(lw["qg"])) * jnp.max(jnp.abs(lw["kg"])) * (ATT_QK_DIM ** 0.5 * 1.02)
    return lax.cond(shift < SHIFT_SAFE_MAX, banded, online, proj)


def _alibi_slopes():
    s = np.array([2.0 ** (-8.0 * (i + 1) / ATT_HEADS) for i in range(ATT_HEADS)], dtype=np.float32)
    return jnp.asarray(np.broadcast_to(s[:, None, None], (ATT_HEADS, 1, LANES)).copy())


def _prep_weights(w):
    row = lambda v: v.reshape(1, -1).astype(F32)
    pad_lanes = lambda v: jnp.pad(v.reshape(1, -1).astype(F32), ((0, 0), (0, LANES - v.size)))
    out = {"layers": [], "slopes": _alibi_slopes()}
    for i in range(DEPTH):
        j = i // 2
        lw = {
            "pre_g": row(w["pre_norm_g"][i]),
            "wg": w["ple_w_gate"][i].astype(BF16),
            "wp": w["ple_w_proj"][i].astype(BF16),
            "gng": row(w["ple_gate_norm_g"][i]),
            "png": row(w["ple_norm_g"][i]),
        }
        if i % 2 == 0:
            win = w["ssd_w_in"][j]
            lw.update(
                w_main=win[:, :SSD_MAIN_COLS].astype(BF16),
                w_dt=jnp.pad(win[:, SSD_MAIN_COLS:], ((0, 0), (0, LANES - 2 * SSD_HEADS))).astype(BF16),
                dt_bias=pad_lanes(w["ssd_dt_bias"][j]),
                a_log=pad_lanes(w["ssd_a_log"][j]),
                conv_w=w["ssd_conv_w"][j].astype(F32),
                conv_b=row(w["ssd_conv_b"][j]),
                d_skip=row(jnp.repeat(w["ssd_d_skip"][j], SSD_HEAD_DIM)),
                norm_g=row(w["ssd_norm_g"][j]),
                w_out=w["ssd_w_out"][j].astype(BF16),
            )
        else:
            lw.update(
                w_in=w["att_w_in"][j].astype(BF16),
                qg=row(jnp.tile(w["att_q_norm_g"][j], 2)),
                kg=row(jnp.tile(w["att_k_norm_g"][j], 2)),
                lam=jnp.stack([w["att_lam_q1"][j], w["att_lam_k1"][j],
                               w["att_lam_q2"][j], w["att_lam_k2"][j]]).astype(F32),
                sub_g=row(w["att_sub_norm_g"][j]),
                w_out=w["att_w_out"][j].astype(BF16),
            )
        out["layers"].append(lw)
    return out


def _trunk(x3, p4, pw):
    nb, seq, d = x3.shape
    t = nb * seq
    x = x3.reshape(t, d)
    for i in range(DEPTH):
        lw = pw["layers"][i]
        p = p4[i].reshape(t, PLE_DIM)
        if i % 2 == 0:
            proj, dt = _inproj(x, lw["pre_g"], lw["w_main"], lw["w_dt"], lw["dt_bias"])
            xs = _conv_silu(proj, lw["conv_w"], lw["conv_b"], nb, seq,
                            SSD_INNER, 0, SSD_INNER, F32)
            bc = _conv_silu(proj, lw["conv_w"], lw["conv_b"], nb, seq,
                            2 * SSD_INNER, SSD_INNER, 2 * SSD_GN, BF16)
            yf = _ssd_scan(xs, bc, dt, lw["a_log"], nb, seq)
            y = _ssd_scan(xs, bc, dt, lw["a_log"], nb, seq, yf, lw["d_skip"])
            x = _ssd_post(y, proj, x, p, lw["norm_g"], lw["w_out"],
                          lw["wg"], lw["wp"], lw["gng"], lw["png"])
        else:
            lambda_init = 0.8 - 0.6 * math.exp(-0.3 * i)
            proj = _inproj(x, lw["pre_g"], lw["w_in"])
            o = _attention(proj, lw, pw["slopes"], nb, seq, lambda_init)
            x = _att_post(o, proj, x, p, lw["w_out"], lw["wg"], lw["wp"], lw["gng"], lw["png"])
    return x.reshape(nb, seq, d)


def kernel(x_prompt, x_sample, p_prompt, p_sample, pre_norm_g, ssd_w_in, ssd_conv_w, ssd_conv_b, ssd_dt_bias, ssd_a_log, ssd_d_skip, ssd_norm_g, ssd_w_out, att_w_in, att_q_norm_g, att_k_norm_g, att_lam_q1, att_lam_k1, att_lam_q2, att_lam_k2, att_sub_norm_g, att_w_out, ple_w_proj, ple_norm_g, ple_gate_norm_g, ple_w_gate):
    pw = _prep_weights(dict(
        pre_norm_g=pre_norm_g, ssd_w_in=ssd_w_in, ssd_conv_w=ssd_conv_w, ssd_conv_b=ssd_conv_b,
        ssd_dt_bias=ssd_dt_bias, ssd_a_log=ssd_a_log, ssd_d_skip=ssd_d_skip, ssd_norm_g=ssd_norm_g,
        ssd_w_out=ssd_w_out, att_w_in=att_w_in, att_q_norm_g=att_q_norm_g, att_k_norm_g=att_k_norm_g,
        att_lam_q1=att_lam_q1, att_lam_k1=att_lam_k1, att_lam_q2=att_lam_q2, att_lam_k2=att_lam_k2,
        att_sub_norm_g=att_sub_norm_g, att_w_out=att_w_out, ple_w_proj=ple_w_proj,
        ple_norm_g=ple_norm_g, ple_gate_norm_g=ple_gate_norm_g, ple_w_gate=ple_w_gate))
    return (_trunk(x_prompt, p_prompt, pw), _trunk(x_sample, p_sample, pw))
```

```python
import functools
import math

import numpy as np
import jax
import jax.numpy as jnp
from jax import lax
from jax.experimental import pallas as pl
from jax.experimental.pallas import tpu as pltpu

F32 = jnp.float32
BF16 = jnp.bfloat16

EPS = 1e-6
D_MODEL = 1024
DEPTH = 4
PLE_DIM = 256

SSD_INNER = 2048
SSD_HEAD_DIM = 64
SSD_HEADS = 32
SSD_GROUPS = 8
SSD_HPG = 4
SSD_STATE = 128
SSD_GN = SSD_GROUPS * SSD_STATE
SSD_CONV_W = 5
SSD_CHUNK = 128
SSD_CHUNKS_PER_STEP = 4
SSD_MAIN_COLS = SSD_INNER + SSD_INNER + 2 * SSD_GN

ATT_HEADS = 8
ATT_QK_DIM = 64
ATT_V_DIM = 128
ATT_IN_COLS = 4096

LANES = 128
SUBLANES = 8
VMEM_LIMIT_BYTES = 56 * 1024 * 1024


def _cparams(*sem):
    return pltpu.CompilerParams(dimension_semantics=sem, vmem_limit_bytes=VMEM_LIMIT_BYTES)


def _tile(n, pref):
    t = min(n, pref)
    assert n % t == 0, (n, pref)
    return t


def _rms(x, g):
    return x * lax.rsqrt(jnp.mean(x * x, axis=-1, keepdims=True) + EPS) * g


def _silu(x):
    return x * jax.nn.sigmoid(x)


def _inproj_kernel(x_ref, g_ref, w_ref, *rest, has_dt):
    if has_dt:
        wdt_ref, dtb_ref, out_ref, dt_ref, hn_ref = rest
    else:
        out_ref, hn_ref = rest

    @pl.when(pl.program_id(1) == 0)
    def _():
        hn_ref[...] = _rms(x_ref[...], g_ref[...]).astype(BF16)
        if has_dt:
            raw = jnp.dot(hn_ref[...], wdt_ref[...], preferred_element_type=F32) + dtb_ref[...]
            dt_ref[...] = jax.nn.softplus(raw)

    out_ref[...] = jnp.dot(hn_ref[...], w_ref[...], preferred_element_type=F32)


def _inproj(x, g, w, wdt=None, dtb=None):
    t, d = x.shape
    n = w.shape[1]
    tm = _tile(t, 1024)
    tn = _tile(n, 2048)
    has_dt = wdt is not None
    in_specs = [
        pl.BlockSpec((tm, d), lambda i, j: (i, 0)),
        pl.BlockSpec((1, d), lambda i, j: (0, 0)),
        pl.BlockSpec((d, tn), lambda i, j: (0, j)),
    ]
    args = [x, g, w]
    out_shape = [jax.ShapeDtypeStruct((t, n), F32)]
    out_specs = [pl.BlockSpec((tm, tn), lambda i, j: (i, j))]
    if has_dt:
        in_specs += [pl.BlockSpec((d, LANES), lambda i, j: (0, 0)),
                     pl.BlockSpec((1, LANES), lambda i, j: (0, 0))]
        args += [wdt, dtb]
        out_shape.append(jax.ShapeDtypeStruct((t, LANES), F32))
        out_specs.append(pl.BlockSpec((tm, LANES), lambda i, j: (i, 0)))
    res = pl.pallas_call(
        functools.partial(_inproj_kernel, has_dt=has_dt),
        grid=(t // tm, n // tn),
        in_specs=in_specs,
        out_specs=out_specs,
        out_shape=out_shape,
        scratch_shapes=[pltpu.VMEM((tm, d), BF16)],
        compiler_params=_cparams("parallel", "arbitrary"),
        name="inproj_dt" if has_dt else "inproj",
    )(*args)
    return res if has_dt else res[0]


def _conv_kernel(prev_ref, cur_ref, next_ref, w_ref, b_ref, out_ref, *, nl):
    i = pl.program_id(1)
    tl = cur_ref.shape[0]
    prev = jnp.where(i > 0, prev_ref[...], 0.0)
    nxt = jnp.where(i < nl - 1, next_ref[...], 0.0)
    ext = jnp.concatenate([prev, cur_ref[...], nxt], axis=0)
    n = tl + 2 * SUBLANES
    acc = jnp.zeros(cur_ref.shape, F32) + b_ref[...]
    half = SSD_CONV_W // 2
    for k in range(SSD_CONV_W):
        shift = (half - k) % n
        r = ext if shift == 0 else pltpu.roll(ext, shift, 0)
        acc = acc + r[SUBLANES:SUBLANES + tl] * w_ref[k:k + 1, :]
    out_ref[...] = _silu(acc).astype(out_ref.dtype)


def _conv_silu(proj, w, b, nb, seq, col0_in, col0_w, ncols, out_dtype):
    t = proj.shape[0]
    cb = 512
    tl = _tile(seq, 512)
    nl = seq // tl
    rb = tl // SUBLANES
    nrb = t // SUBLANES
    ci, cw = col0_in // cb, col0_w // cb
    return pl.pallas_call(
        functools.partial(_conv_kernel, nl=nl),
        grid=(nb, nl, ncols // cb),
        in_specs=[
            pl.BlockSpec((SUBLANES, cb), lambda bb, i, c: (jnp.maximum((bb * nl + i) * rb - 1, 0), ci + c)),
            pl.BlockSpec((tl, cb), lambda bb, i, c: (bb * nl + i, ci + c)),
            pl.BlockSpec((SUBLANES, cb), lambda bb, i, c: (jnp.minimum((bb * nl + i + 1) * rb, nrb - 1), ci + c)),
            pl.BlockSpec((SSD_CONV_W, cb), lambda bb, i, c: (0, cw + c)),
            pl.BlockSpec((1, cb), lambda bb, i, c: (0, cw + c)),
        ],
        out_specs=pl.BlockSpec((tl, cb), lambda bb, i, c: (bb * nl + i, c)),
        out_shape=jax.ShapeDtypeStruct((t, ncols), out_dtype),
        compiler_params=_cparams("parallel", "parallel", "parallel"),
        name="conv_silu",
    )(proj, proj, proj, w, b)


def _ssd_spread_table(reverse):
    lane0 = SSD_HEADS if reverse else 0
    ee = np.zeros((2 * LANES, SSD_INNER), np.float32)
    for h in range(SSD_HEADS):
        for piece in range(2):
            ee[piece * LANES + lane0 + h, h * SSD_HEAD_DIM:(h + 1) * SSD_HEAD_DIM] = 1.0
    return jnp.asarray(ee, BF16)


def _ssd_kernel(xs_ref, b_ref, c_ref, dt_ref, alog_ref, ee_ref, *rest, reverse):
    q = SSD_CHUNK
    gw = SSD_HPG * SSD_HEAD_DIM
    if reverse:
        yf_ref, dsk_ref, y_ref, state_ref = rest
    else:
        y_ref, state_ref = rest

    @pl.when(pl.program_id(1) == 0)
    def _():
        state_ref[...] = jnp.zeros(state_ref.shape, F32)

    lane0 = SSD_HEADS if reverse else 0
    row = lax.broadcasted_iota(jnp.int32, (q, q), 0)
    col = lax.broadcasted_iota(jnp.int32, (q, q), 1)
    mask = (row <= col) if reverse else (row >= col)
    tri = jnp.where(mask, 1.0, 0.0).astype(F32)
    head_of_lane = lax.broadcasted_iota(jnp.int32, (q, gw), 1) // SSD_HEAD_DIM

    pieces2 = lambda x: jnp.concatenate(_split2(x), axis=1).astype(BF16)

    def chunk(rows):
        dtc = dt_ref[rows]
        a = dtc * (-jnp.exp(alog_ref[...]))
        c = jnp.dot(tri, a, precision=lax.Precision.HIGHEST, preferred_element_type=F32)
        c_t = c.T
        tot = c[0:1, :] if reverse else c[q - 1:q, :]
        ec = jnp.exp(c)
        ed = jnp.exp(tot - c)

        lhs_w = jnp.concatenate([pieces2(dtc), pieces2(dtc * ed), pieces2(ec)], axis=0)
        spread = jnp.dot(lhs_w, ee_ref[...], preferred_element_type=F32)

        for g in range(SSD_GROUPS):
            gs = slice(g * gw, (g + 1) * gw)
            bg = b_ref[rows, g * SSD_STATE:(g + 1) * SSD_STATE]
            cg = c_ref[rows, g * SSD_STATE:(g + 1) * SSD_STATE]
            dt_e, dted_e, ec_e = spread[0:q, gs], spread[q:2 * q, gs], spread[2 * q:3 * q, gs]
            xs_g = xs_ref[rows, gs]
            xdt = (xs_g * dt_e).astype(BF16)
            xw = (xs_g * dted_e).astype(BF16)
            cb = lax.dot_general(cg, bg, (((1,), (1,)), ((), ())), preferred_element_type=F32)
            bg_t = bg.astype(F32).T.astype(BF16)
            st_new = jnp.dot(bg_t, xw, preferred_element_type=F32)
            prev = state_ref[g]
            y_off = jnp.dot(cg, prev.astype(BF16), preferred_element_type=F32) * ec_e
            etot_e = ec_e[0:1, :] if reverse else ec_e[q - 1:q, :]
            state_ref[g] = prev * etot_e + st_new
            ms, xd = [], []
            for r in range(SSD_HPG):
                h = g * SSD_HPG + r
                seg = c[:, lane0 + h:lane0 + h + 1] - c_t[lane0 + h:lane0 + h + 1, :]
                lmat = jnp.exp(jnp.where(mask, seg, -jnp.inf))
                ms.append((cb * lmat).astype(BF16))
                xd.append(jnp.where(head_of_lane == r, xdt, jnp.zeros_like(xdt)))
            y_diag = jnp.dot(jnp.concatenate(ms, axis=1), jnp.concatenate(xd, axis=0),
                             preferred_element_type=F32)
            y_g = y_diag + y_off
            if reverse:
                y_g = yf_ref[rows, gs] + y_g + xs_g * dsk_ref[:, gs]
            y_ref[rows, gs] = y_g

    n_chunks = dt_ref.shape[0] // q
    for ci in (reversed(range(n_chunks)) if reverse else range(n_chunks)):
        chunk(slice(ci * q, (ci + 1) * q))


def _ssd_scan(xs, bc, dt, alog, nb, seq, yf=None, dsk=None):
    reverse = yf is not None
    t = xs.shape[0]
    q = _tile(seq, SSD_CHUNKS_PER_STEP * SSD_CHUNK)
    nc = seq // q
    ee = _ssd_spread_table(reverse)

    def rblk(bb, j):
        return bb * nc + ((nc - 1 - j) if reverse else j)

    row_spec = pl.BlockSpec((q, SSD_INNER), lambda bb, j: (rblk(bb, j), 0))
    in_specs = [
        row_spec,
        pl.BlockSpec((q, SSD_GN), lambda bb, j: (rblk(bb, j), 0)),
        pl.BlockSpec((q, SSD_GN), lambda bb, j: (rblk(bb, j), 1)),
        pl.BlockSpec((q, LANES), lambda bb, j: (rblk(bb, j), 0)),
        pl.BlockSpec((1, LANES), lambda bb, j: (0, 0)),
        pl.BlockSpec(ee.shape, lambda bb, j: (0, 0)),
    ]
    args = [xs, bc, bc, dt, alog, ee]
    if reverse:
        in_specs += [row_spec, pl.BlockSpec((1, SSD_INNER), lambda bb, j: (0, 0))]
        args += [yf, dsk]
    return pl.pallas_call(
        functools.partial(_ssd_kernel, reverse=reverse),
        grid=(nb, nc),
        in_specs=in_specs,
        out_specs=row_spec,
        out_shape=jax.ShapeDtypeStruct((t, SSD_INNER), F32),
        scratch_shapes=[pltpu.VMEM((SSD_GROUPS, SSD_STATE, SSD_HPG * SSD_HEAD_DIM), F32)],
        compiler_params=_cparams("parallel", "arbitrary"),
        name="ssd_scan_bwd" if reverse else "ssd_scan_fwd",
    )(*args)


def _ple_tail(x, mix, p_ref, wg_ref, wp_ref, gng_ref, png_ref, out_ref):
    x1 = x + mix
    gate = jax.nn.sigmoid(
        jnp.dot(_rms(x1, gng_ref[...]).astype(BF16), wg_ref[...], preferred_element_type=F32))
    e = _rms(jnp.dot(p_ref[...].astype(BF16), wp_ref[...], preferred_element_type=F32), png_ref[...])
    out_ref[...] = x1 + gate * e


def _ssd_post_kernel(y_ref, z_ref, x_ref, p_ref, ng_ref, wo_ref,
                     wg_ref, wp_ref, gng_ref, png_ref, out_ref):
    y = _rms(y_ref[...] * _silu(z_ref[...]), ng_ref[...])
    mix = jnp.dot(y.astype(BF16), wo_ref[...], preferred_element_type=F32)
    _ple_tail(x_ref[...], mix, p_ref, wg_ref, wp_ref, gng_ref, png_ref, out_ref)


def _att_post_kernel(o_ref, gate_ref, x_ref, p_ref, wo_ref, wg_ref, wp_ref, gng_ref, png_ref, out_ref):
    u = o_ref[...] * _silu(gate_ref[...])
    mix = jnp.dot(u.astype(BF16), wo_ref[...], preferred_element_type=F32)
    _ple_tail(x_ref[...], mix, p_ref, wg_ref, wp_ref, gng_ref, png_ref, out_ref)


def _row_spec(tm, width, colblk=0):
    return pl.BlockSpec((tm, width), lambda i: (i, colblk))


def _full_spec(shape):
    return pl.BlockSpec(shape, lambda i: (0,) * len(shape))


def _ssd_post(y, proj, x, p, ng, wo, wg, wp, gng, png):
    t = x.shape[0]
    tm = _tile(t, 512)
    return pl.pallas_call(
        _ssd_post_kernel,
        grid=(t // tm,),
        in_specs=[
            _row_spec(tm, SSD_INNER),
            _row_spec(tm, SSD_INNER, 0),
            _row_spec(tm, D_MODEL), _row_spec(tm, PLE_DIM),
            _full_spec((1, SSD_INNER)),
            _full_spec(wo.shape), _full_spec(wg.shape), _full_spec(wp.shape),
            _full_spec((1, D_MODEL)), _full_spec((1, D_MODEL)),
        ],
        out_specs=_row_spec(tm, D_MODEL),
        out_shape=jax.ShapeDtypeStruct((t, D_MODEL), F32),
        compiler_params=_cparams("parallel"),
        name="ssd_post",
    )(y, proj, x, p, ng, wo, wg, wp, gng, png)


def _att_post(o, proj, x, p, wo, wg, wp, gng, png):
    t = x.shape[0]
    tm = _tile(t, 512)
    return pl.pallas_call(
        _att_post_kernel,
        grid=(t // tm,),
        in_specs=[
            _row_spec(tm, D_MODEL),
            _row_spec(tm, D_MODEL, 3),
            _row_spec(tm, D_MODEL), _row_spec(tm, PLE_DIM),
            _full_spec(wo.shape), _full_spec(wg.shape), _full_spec(wp.shape),
            _full_spec((1, D_MODEL)), _full_spec((1, D_MODEL)),
        ],
        out_specs=_row_spec(tm, D_MODEL),
        out_shape=jax.ShapeDtypeStruct((t, D_MODEL), F32),
        compiler_params=_cparams("parallel"),
        name="att_post",
    )(o, proj, x, p, wo, wg, wp, gng, png)


def _attprep_kernel(q_ref, k_ref, v_ref, qg_ref, kg_ref, qn_ref, kt_ref, vb_ref):
    shape = q_ref.shape
    first = lax.broadcasted_iota(jnp.int32, shape, 1) < ATT_QK_DIM

    def norm(x, g):
        x2 = x * x
        s1 = jnp.sum(jnp.where(first, x2, 0.0), axis=-1, keepdims=True)
        s2 = jnp.sum(jnp.where(first, 0.0, x2), axis=-1, keepdims=True)
        ms = jnp.where(first, s1, s2) * (1.0 / ATT_QK_DIM)
        return x * lax.rsqrt(ms + EPS) * g

    qn_ref[0] = (norm(q_ref[...], qg_ref[...]) * (ATT_QK_DIM ** -0.5)).astype(BF16)
    kt_ref[0] = norm(k_ref[...], kg_ref[...]).T.astype(BF16)
    vb_ref[0] = v_ref[...].astype(BF16)


def _att_prep(proj, qg, kg, nb, seq):
    tl = _tile(seq, 512)
    nl = seq // tl
    nh = ATT_HEADS
    blk = lambda off: pl.BlockSpec((tl, LANES), lambda bb, h, i: (bb * nl + i, off + h))
    return pl.pallas_call(
        _attprep_kernel,
        grid=(nb, nh, nl),
        in_specs=[blk(0), blk(nh), blk(2 * nh), _g3((1, LANES)), _g3((1, LANES))],
        out_specs=[
            pl.BlockSpec((1, tl, LANES), lambda bb, h, i: (bb * nh + h, i, 0)),
            pl.BlockSpec((1, LANES, tl), lambda bb, h, i: (bb * nh + h, 0, i)),
            pl.BlockSpec((1, tl, LANES), lambda bb, h, i: (bb * nh + h, i, 0)),
        ],
        out_shape=[
            jax.ShapeDtypeStruct((nb * nh, seq, LANES), BF16),
            jax.ShapeDtypeStruct((nb * nh, LANES, seq), BF16),
            jax.ShapeDtypeStruct((nb * nh, seq, LANES), BF16),
        ],
        compiler_params=_cparams("parallel", "parallel", "parallel"),
        name="att_prep",
    )(proj, proj, proj, qg, kg)


def _g3(shape):
    return pl.BlockSpec(shape, lambda bb, h, i: (0,) * len(shape))


def _flash_kernel(q_ref, kt_ref, v_ref, slope_ref, lam_ref, sg_ref, o_ref, *, tk, nk, lambda_init):
    tq = q_ref.shape[1]
    q = q_ref[0]
    first = lax.broadcasted_iota(jnp.int32, q.shape, 1) < ATT_QK_DIM
    zero = jnp.zeros_like(q)
    q1 = jnp.where(first, q, zero)
    q2 = jnp.where(first, zero, q)
    slope = slope_ref[0][:, 0:1]
    rel = (lax.broadcasted_iota(jnp.int32, (tq, tk), 1)
           - lax.broadcasted_iota(jnp.int32, (tq, tk), 0))
    q0 = pl.program_id(2) * tq

    def update(qm, kt, vt, bias, m, l, acc):
        s = jnp.dot(qm, kt, preferred_element_type=F32) - bias
        m_new = jnp.maximum(m, jnp.max(s, axis=-1, keepdims=True))
        alpha = jnp.exp(m - m_new)
        p = jnp.exp(s - m_new)
        l = alpha * l + jnp.sum(p, axis=-1, keepdims=True)
        acc = alpha * acc + jnp.dot(p.astype(BF16), vt, preferred_element_type=F32)
        return m_new, l, acc

    def body(kidx, carry):
        m1, l1, a1, m2, l2, a2 = carry
        k0 = pl.multiple_of(kidx * tk, tk)
        kt = kt_ref[0, :, pl.ds(k0, tk)]
        vt = v_ref[0, pl.ds(k0, tk), :]
        bias = jnp.abs(rel + (k0 - q0)).astype(F32) * slope
        m1, l1, a1 = update(q1, kt, vt, bias, m1, l1, a1)
        m2, l2, a2 = update(q2, kt, vt, bias, m2, l2, a2)
        return m1, l1, a1, m2, l2, a2

    neg = jnp.full((tq, 1), -jnp.inf, F32)
    zl = jnp.zeros((tq, 1), F32)
    za = jnp.zeros((tq, ATT_V_DIM), F32)
    m1, l1, a1, m2, l2, a2 = lax.fori_loop(0, nk, body, (neg, zl, za, neg, zl, za))

    lv = lam_ref[...]
    lam = (jnp.exp(jnp.sum(lv[0:1] * lv[1:2], axis=-1, keepdims=True))
           - jnp.exp(jnp.sum(lv[2:3] * lv[3:4], axis=-1, keepdims=True)) + lambda_init)
    o = a1 / l1 - lam * (a2 / l2)
    o_ref[...] = _rms(o, sg_ref[...]) * (1.0 - lambda_init)


def _flash(qn, kt, vb, slopes, lamv, sg, nb, seq, lambda_init):
    nh = ATT_HEADS
    tq = _tile(seq, 512)
    tk = _tile(seq, 512)
    nq = seq // tq
    return pl.pallas_call(
        functools.partial(_flash_kernel, tk=tk, nk=seq // tk, lambda_init=lambda_init),
        grid=(nb, nh, nq),
        in_specs=[
            pl.BlockSpec((1, tq, LANES), lambda bb, h, i: (bb * nh + h, i, 0)),
            pl.BlockSpec((1, LANES, seq), lambda bb, h, i: (bb * nh + h, 0, 0)),
            pl.BlockSpec((1, seq, LANES), lambda bb, h, i: (bb * nh + h, 0, 0)),
            pl.BlockSpec((1, 1, LANES), lambda bb, h, i: (h, 0, 0)),
            _g3((4, ATT_QK_DIM)),
            _g3((1, ATT_V_DIM)),
        ],
        out_specs=pl.BlockSpec((tq, LANES), lambda bb, h, i: (bb * nq + i, h)),
        out_shape=jax.ShapeDtypeStruct((nb * seq, nh * ATT_V_DIM), F32),
        compiler_params=_cparams("parallel", "parallel", "arbitrary"),
        name="flash_diff_attn",
    )(qn, kt, vb, slopes, lamv, sg)


FLASH_TILE = 512
POS_U = 0
POS_W_LOW = 2
POS_OFF = 4
POS_SHIFT = 7
POS_W_UP = 8
BAND_EXP_CUTOFF = 112.0
SHIFT_SAFE_MAX = 40.0


def _split2(x):
    hi = x.astype(BF16).astype(F32)
    lo = (x - hi).astype(BF16).astype(F32)
    return hi, lo


def _split3(x):
    hi = x.astype(BF16).astype(F32)
    r = x - hi
    mid = r.astype(BF16).astype(F32)
    lo = (r - mid).astype(BF16).astype(F32)
    return hi, mid, lo


def _attprep_band_kernel(q_ref, k_ref, v_ref, qg_ref, kg_ref, slope_ref, qn_ref, kt_ref, vb_ref):
    tl = q_ref.shape[0]
    gr = lax.broadcasted_iota(jnp.int32, (3 * LANES, LANES), 0) % LANES
    gc = lax.broadcasted_iota(jnp.int32, (3 * LANES, LANES), 1)
    same_half = jnp.where((gr < ATT_QK_DIM) == (gc < ATT_QK_DIM), 1.0, 0.0).astype(BF16)

    def norm(x, g):
        pieces = jnp.concatenate(_split3(x * x), axis=1).astype(BF16)
        ms = jnp.dot(pieces, same_half, preferred_element_type=F32) * (1.0 / ATT_QK_DIM)
        return x * lax.rsqrt(ms + EPS) * g

    mq = jnp.max(jnp.abs(qg_ref[...]), axis=-1, keepdims=True)
    mk = jnp.max(jnp.abs(kg_ref[...]), axis=-1, keepdims=True)
    shift = mq * mk * (ATT_QK_DIM ** 0.5 * 1.02)

    r = lax.broadcasted_iota(jnp.int32, (LANES, tl), 0)
    j = lax.broadcasted_iota(jnp.int32, (LANES, tl), 1).astype(F32)
    minus_one = (r == POS_U) | (r == POS_U + 1) | ((r >= POS_OFF) & (r < POS_OFF + 3))
    pos_const = jnp.where(r == POS_SHIFT, -shift, jnp.where(minus_one, -1.0, 0.0))
    ones = jnp.ones((tl, LANES), BF16)

    for h in range(ATT_HEADS):
        hs = slice(h * LANES, (h + 1) * LANES)
        qn_ref[h] = (norm(q_ref[:, hs], qg_ref[...]) * (ATT_QK_DIM ** -0.5)).astype(BF16)
        kt_ref[h, 0:LANES, :] = norm(k_ref[:, hs], kg_ref[...]).T.astype(BF16)
        slope = slope_ref[h][:, 0:1]
        wl_hi, wl_lo = _split2(slope * (tl - 1.0 - j))
        wu_hi, wu_lo = _split2(slope * j)
        pos = jnp.where(r == POS_W_LOW, -wl_hi, pos_const)
        pos = jnp.where(r == POS_W_LOW + 1, -wl_lo, pos)
        pos = jnp.where(r == POS_W_UP, -wu_hi, pos)
        pos = jnp.where(r == POS_W_UP + 1, -wu_lo, pos)
        kt_ref[h, LANES:2 * LANES, :] = pos.astype(BF16)
        vb_ref[h, :, 0:LANES] = v_ref[:, hs].astype(BF16)
        vb_ref[h, :, LANES:2 * LANES] = ones


def _att_prep_band(proj, qg, kg, slopes, nb, seq):
    tl = _tile(seq, FLASH_TILE)
    nl = seq // tl
    nh = ATT_HEADS
    w = nh * LANES
    blk = lambda c: pl.BlockSpec((tl, w), lambda bb, i: (bb * nl + i, c))
    full = lambda shape: pl.BlockSpec(shape, lambda bb, i: (0,) * len(shape))
    return pl.pallas_call(
        _attprep_band_kernel,
        grid=(nb, nl),
        in_specs=[blk(0), blk(1), blk(2), full((1, LANES)), full((1, LANES)),
                  full((nh, 1, LANES))],
        out_specs=[
            pl.BlockSpec((nh, tl, LANES), lambda bb, i: (bb, i, 0)),
            pl.BlockSpec((nh, 2 * LANES, tl), lambda bb, i: (bb, 0, i)),
            pl.BlockSpec((nh, tl, 2 * LANES), lambda bb, i: (bb, i, 0)),
        ],
        out_shape=[
            jax.ShapeDtypeStruct((nb * nh, seq, LANES), BF16),
            jax.ShapeDtypeStruct((nb * nh, 2 * LANES, seq), BF16),
            jax.ShapeDtypeStruct((nb * nh, seq, 2 * LANES), BF16),
        ],
        compiler_params=_cparams("parallel", "parallel"),
        name="att_prep_band",
    )(proj, proj, proj, qg, kg, slopes)


def _flash_band_kernel(q_ref, kt_ref, v_ref, slope_ref, lam_ref, sg_ref, o_ref,
                       lhs_ref, acc_ref, p_ref, base_ref, bias_ref, *, tk, nk, lambda_init):
    tq = q_ref.shape[1]
    assert tq == tk
    h = pl.program_id(1)
    qi = pl.program_id(2)
    q0 = qi * tq
    slope_row = slope_ref[0]
    lane = lax.broadcasted_iota(jnp.int32, (tq, LANES), 1)
    rowf = lax.broadcasted_iota(jnp.int32, (tq, LANES), 0).astype(F32)

    q = q_ref[0]
    zero = jnp.zeros_like(q)
    first = lane < ATT_QK_DIM
    lhs_ref[0:tq, 0:LANES] = jnp.where(first, q, zero)
    lhs_ref[tq:2 * tq, 0:LANES] = jnp.where(first, zero, q)
    acc_ref[...] = jnp.zeros(acc_ref.shape, F32)

    def pos_base(u, w_lane):
        u_hi, u_lo = _split2(u)
        ones = (lane == POS_SHIFT) | (lane == w_lane) | (lane == w_lane + 1)
        base = jnp.where(ones, 1.0, 0.0)
        base = jnp.where(lane == POS_U, u_hi, base)
        return jnp.where(lane == POS_U + 1, u_lo, base)

    off_lanes = (lane >= POS_OFF) & (lane < POS_OFF + 3)
    lane1 = lax.broadcasted_iota(jnp.int32, (2 * SUBLANES, LANES), 1)
    slope_tile = jnp.broadcast_to(slope_row, (2 * SUBLANES, LANES))

    def set_pos(base, dist):
        if dist is not None:
            hi, mid, lo = _split3(slope_tile * dist.astype(F32))
            orow = jnp.where(lane1 == POS_OFF, hi, jnp.where(lane1 == POS_OFF + 1, mid, lo))
            base = jnp.where(off_lanes, orow[0:1, :], base)
        pos = base.astype(BF16)
        lhs_ref[0:tq, LANES:2 * LANES] = pos
        lhs_ref[tq:2 * tq, LANES:2 * LANES] = pos

    halves = (slice(0, tq), slice(tq, 2 * tq))

    def scores(kidx, slot):
        k0 = pl.multiple_of(kidx * tk, tk)
        s = jnp.dot(lhs_ref[...], kt_ref[0, :, pl.ds(k0, tk)], preferred_element_type=F32)
        p_ref[slot] = jnp.exp(s).astype(BF16)

    def weighted_sum(kidx, slot):
        k0 = pl.multiple_of(kidx * tk, tk)
        acc_ref[...] += jnp.dot(p_ref[slot], v_ref[0, pl.ds(k0, tk), :], preferred_element_type=F32)


    band = lax.shift_left(jnp.int32(int(BAND_EXP_CUTOFF)), h + 1)
    k_lo = jnp.maximum(q0 + 1 - band, 0) // tk
    k_hi = jnp.minimum((q0 + tq - 2 + band + tk) // tk, nk)
    n_low = qi - k_lo
    n_off_diag = n_low + (k_hi - qi - 1)

    @pl.when(qi == 0)
    def _():
        base_ref[0] = pos_base(slope_row * rowf, POS_W_LOW)
        base_ref[1] = pos_base(slope_row * (tq - 1.0 - rowf), POS_W_UP)
        rel = (lax.broadcasted_iota(jnp.int32, (tq, tk), 1)
               - lax.broadcasted_iota(jnp.int32, (tq, tk), 0))
        bias_ref[...] = jnp.abs(rel).astype(F32) * slope_row[:, 0:1]

    set_pos(jnp.where(lane == POS_SHIFT, 1.0, 0.0), None)
    s = jnp.dot(lhs_ref[...], kt_ref[0, :, pl.ds(pl.multiple_of(q0, tk), tk)],
                preferred_element_type=F32)
    for rows in halves:
        p_ref[0, rows] = jnp.exp(s[rows] - bias_ref[...]).astype(BF16)

    def tile_of(u):
        return jnp.where(u < n_low, k_lo + u, qi + 1 + u - n_low)

    def step(u, prev, slot):
        below = u < n_low
        kidx = tile_of(u)
        dist = jnp.where(below, q0 - kidx * tk - (tk - 1), kidx * tk - q0 - (tq - 1))
        weighted_sum(prev, slot)
        set_pos(base_ref[jnp.where(below, 0, 1)], dist)
        scores(kidx, 1 - slot)
        return kidx

    def pair_body(i, prev):
        return step(2 * i + 1, step(2 * i, prev, 0), 1)

    n_pairs = n_off_diag // 2
    prev = lax.fori_loop(0, n_pairs, pair_body, qi)
    odd = n_off_diag - 2 * n_pairs

    @pl.when(odd == 1)
    def _():
        weighted_sum(step(n_off_diag - 1, prev, 0), 1)

    @pl.when(odd == 0)
    def _():
        weighted_sum(prev, 0)

    lv = lam_ref[...]
    lam = (jnp.exp(jnp.sum(lv[0:1] * lv[1:2], axis=-1, keepdims=True))
           - jnp.exp(jnp.sum(lv[2:3] * lv[3:4], axis=-1, keepdims=True)) + lambda_init)
    a1 = acc_ref[0:tq, 0:LANES]
    l1 = acc_ref[0:tq, LANES:2 * LANES]
    a2 = acc_ref[tq:2 * tq, 0:LANES]
    l2 = acc_ref[tq:2 * tq, LANES:2 * LANES]
    o = a1 / l1 - lam * (a2 / l2)
    o_ref[...] = _rms(o, sg_ref[...]) * (1.0 - lambda_init)


def _flash_band(qn, kt, vb, slopes, lamv, sg, nb, seq, lambda_init):
    nh = ATT_HEADS
    tq = _tile(seq, FLASH_TILE)
    nq = seq // tq
    return pl.pallas_call(
        functools.partial(_flash_band_kernel, tk=tq, nk=nq, lambda_init=lambda_init),
        grid=(nb, nh, nq),
        in_specs=[
            pl.BlockSpec((1, tq, LANES), lambda bb, h, i: (bb * nh + h, i, 0)),
            pl.BlockSpec((1, 2 * LANES, seq), lambda bb, h, i: (bb * nh + h, 0, 0)),
            pl.BlockSpec((1, seq, 2 * LANES), lambda bb, h, i: (bb * nh + h, 0, 0)),
            pl.BlockSpec((1, 1, LANES), lambda bb, h, i: (h, 0, 0)),
            _g3((4, ATT_QK_DIM)),
            _g3((1, ATT_V_DIM)),
        ],
        out_specs=pl.BlockSpec((tq, LANES), lambda bb, h, i: (bb * nq + i, h)),
        out_shape=jax.ShapeDtypeStruct((nb * seq, nh * ATT_V_DIM), F32),
        scratch_shapes=[
            pltpu.VMEM((2 * tq, 2 * LANES), BF16),
            pltpu.VMEM((2 * tq, 2 * LANES), F32),
            pltpu.VMEM((2, 2 * tq, tq), BF16),
            pltpu.VMEM((2, tq, LANES), F32),
            pltpu.VMEM((tq, tq), F32),
        ],
        compiler_params=_cparams("parallel", "parallel", "arbitrary"),
        name="flash_band",
    )(qn, kt, vb, slopes, lamv, sg)


def _attention(proj, lw, slopes, nb, seq, lambda_init):
    def banded(proj):
        qn, kt, vb = _att_prep_band(proj, lw["qg"], lw["kg"], slopes, nb, seq)
        return _flash_band(qn, kt, vb, slopes, lw["lam"], lw["sub_g"], nb, seq, lambda_init)

    def online(proj):
        qn, kt, vb = _att_prep(proj, lw["qg"], lw["kg"], nb, seq)
        return _flash(qn, kt, vb, slopes, lw["lam"], lw["sub_g"], nb, seq, lambda_init)

    shift = jnp.max(jnp.abs(lw["qg"])) * jnp.max(jnp.abs(lw["kg"])) * (ATT_QK_DIM ** 0.5 * 1.02)
    return lax.cond(shift < SHIFT_SAFE_MAX, banded, online, proj)


def _alibi_slopes():
    s = np.array([2.0 ** (-8.0 * (i + 1) / ATT_HEADS) for i in range(ATT_HEADS)], dtype=np.float32)
    return jnp.asarray(np.broadcast_to(s[:, None, None], (ATT_HEADS, 1, LANES)).copy())


def _prep_weights(w):
    row = lambda v: v.reshape(1, -1).astype(F32)
    pad_lanes = lambda v: jnp.pad(v.reshape(1, -1).astype(F32), ((0, 0), (0, LANES - v.size)))
    out = {"layers": [], "slopes": _alibi_slopes()}
    for i in range(DEPTH):
        j = i // 2
        lw = {
            "pre_g": row(w["pre_norm_g"][i]),
            "wg": w["ple_w_gate"][i].astype(BF16),
            "wp": w["ple_w_proj"][i].astype(BF16),
            "gng": row(w["ple_gate_norm_g"][i]),
            "png": row(w["ple_norm_g"][i]),
        }
        if i % 2 == 0:
            win = w["ssd_w_in"][j]
            lw.update(
                w_main=win[:, :SSD_MAIN_COLS].astype(BF16),
                w_dt=jnp.pad(win[:, SSD_MAIN_COLS:], ((0, 0), (0, LANES - 2 * SSD_HEADS))).astype(BF16),
                dt_bias=pad_lanes(w["ssd_dt_bias"][j]),
                a_log=pad_lanes(w["ssd_a_log"][j]),
                conv_w=w["ssd_conv_w"][j].astype(F32),
                conv_b=row(w["ssd_conv_b"][j]),
                d_skip=row(jnp.repeat(w["ssd_d_skip"][j], SSD_HEAD_DIM)),
                norm_g=row(w["ssd_norm_g"][j]),
                w_out=w["ssd_w_out"][j].astype(BF16),
            )
        else:
            lw.update(
                w_in=w["att_w_in"][j].astype(BF16),
                qg=row(jnp.tile(w["att_q_norm_g"][j], 2)),
                kg=row(jnp.tile(w["att_k_norm_g"][j], 2)),
                lam=jnp.stack([w["att_lam_q1"][j], w["att_lam_k1"][j],
                               w["att_lam_q2"][j], w["att_lam_k2"][j]]).astype(F32),
                sub_g=row(w["att_sub_norm_g"][j]),
                w_out=w["att_w_out"][j].astype(BF16),
            )
        out["layers"].append(lw)
    return out


def _trunk(x3, p4, pw):
    nb, seq, d = x3.shape
    t = nb * seq
    x = x3.reshape(t, d)
    for i in range(DEPTH):
        lw = pw["layers"][i]
        p = p4[i].reshape(t, PLE_DIM)
        if i % 2 == 0:
            proj, dt = _inproj(x, lw["pre_g"], lw["w_main"], lw["w_dt"], lw["dt_bias"])
            xs = _conv_silu(proj, lw["conv_w"], lw["conv_b"], nb, seq,
                            SSD_INNER, 0, SSD_INNER, F32)
            bc = _conv_silu(proj, lw["conv_w"], lw["conv_b"], nb, seq,
                            2 * SSD_INNER, SSD_INNER, 2 * SSD_GN, BF16)
            yf = _ssd_scan(xs, bc, dt, lw["a_log"], nb, seq)
            y = _ssd_scan(xs, bc, dt, lw["a_log"], nb, seq, yf, lw["d_skip"])
            x = _ssd_post(y, proj, x, p, lw["norm_g"], lw["w_out"],
                          lw["wg"], lw["wp"], lw["gng"], lw["png"])
        else:
            lambda_init = 0.8 - 0.6 * math.exp(-0.3 * i)
            proj = _inproj(x, lw["pre_g"], lw["w_in"])
            o = _attention(proj, lw, pw["slopes"], nb, seq, lambda_init)
            x = _att_post(o, proj, x, p, lw["w_out"], lw["wg"], lw["wp"], lw["gng"], lw["png"])
    return x.reshape(nb, seq, d)


def kernel(x_prompt, x_sample, p_prompt, p_sample, pre_norm_g, ssd_w_in, ssd_conv_w, ssd_conv_b, ssd_dt_bias, ssd_a_log, ssd_d_skip, ssd_norm_g, ssd_w_out, att_w_in, att_q_norm_g, att_k_norm_g, att_lam_q1, att_lam_k1, att_lam_q2, att_lam_k2, att_sub_norm_g, att_w_out, ple_w_proj, ple_norm_g, ple_gate_norm_g, ple_w_gate):
    pw = _prep_weights(dict(
        pre_norm_g=pre_norm_g, ssd_w_in=ssd_w_in, ssd_conv_w=ssd_conv_w, ssd_conv_b=ssd_conv_b,
        ssd_dt_bias=ssd_dt_bias, ssd_a_log=ssd_a_log, ssd_d_skip=ssd_d_skip, ssd_norm_g=ssd_norm_g,
        ssd_w_out=ssd_w_out, att_w_in=att_w_in, att_q_norm_g=att_q_norm_g, att_k_norm_g=att_k_norm_g,
        att_lam_q1=att_lam_q1, att_lam_k1=att_lam_k1, att_lam_q2=att_lam_q2, att_lam_k2=att_lam_k2,
        att_sub_norm_g=att_sub_norm_g, att_w_out=att_w_out, ple_w_proj=ple_w_proj,
        ple_norm_g=ple_norm_g, ple_gate_norm_g=ple_gate_norm_g, ple_w_gate=ple_w_gate))
    return (_trunk(x_prompt, p_prompt, pw), _trunk(x_sample, p_sample, pw))
```

```python
import functools
import math

import numpy as np
import jax
import jax.numpy as jnp
from jax import lax
from jax.experimental import pallas as pl
from jax.experimental.pallas import tpu as pltpu

F32 = jnp.float32
BF16 = jnp.bfloat16

EPS = 1e-6
D_MODEL = 1024
DEPTH = 4
PLE_DIM = 256

SSD_INNER = 2048
SSD_HEAD_DIM = 64
SSD_HEADS = 32
SSD_GROUPS = 8
SSD_HPG = 4
SSD_STATE = 128
SSD_GN = SSD_GROUPS * SSD_STATE
SSD_CONV_W = 5
SSD_CHUNK = 128
SSD_CHUNKS_PER_STEP = 4
SSD_MAIN_COLS = SSD_INNER + SSD_INNER + 2 * SSD_GN

ATT_HEADS = 8
ATT_QK_DIM = 64
ATT_V_DIM = 128
ATT_IN_COLS = 4096

LANES = 128
SUBLANES = 8
VMEM_LIMIT_BYTES = 56 * 1024 * 1024


def _cparams(*sem):
    return pltpu.CompilerParams(dimension_semantics=sem, vmem_limit_bytes=VMEM_LIMIT_BYTES)


def _tile(n, pref):
    t = min(n, pref)
    assert n % t == 0, (n, pref)
    return t


def _rms(x, g):
    return x * lax.rsqrt(jnp.mean(x * x, axis=-1, keepdims=True) + EPS) * g


def _silu(x):
    return x * jax.nn.sigmoid(x)


def _inproj_kernel(x_ref, g_ref, w_ref, *rest, has_dt):
    if has_dt:
        wdt_ref, dtb_ref, out_ref, dt_ref, hn_ref = rest
    else:
        out_ref, hn_ref = rest

    @pl.when(pl.program_id(1) == 0)
    def _():
        hn_ref[...] = _rms(x_ref[...], g_ref[...]).astype(BF16)
        if has_dt:
            raw = jnp.dot(hn_ref[...], wdt_ref[...], preferred_element_type=F32) + dtb_ref[...]
            dt_ref[...] = jax.nn.softplus(raw)

    out_ref[...] = jnp.dot(hn_ref[...], w_ref[...], preferred_element_type=F32)


def _inproj(x, g, w, wdt=None, dtb=None):
    t, d = x.shape
    n = w.shape[1]
    tm = _tile(t, 1024)
    tn = _tile(n, 2048)
    has_dt = wdt is not None
    in_specs = [
        pl.BlockSpec((tm, d), lambda i, j: (i, 0)),
        pl.BlockSpec((1, d), lambda i, j: (0, 0)),
        pl.BlockSpec((d, tn), lambda i, j: (0, j)),
    ]
    args = [x, g, w]
    out_shape = [jax.ShapeDtypeStruct((t, n), F32)]
    out_specs = [pl.BlockSpec((tm, tn), lambda i, j: (i, j))]
    if has_dt:
        in_specs += [pl.BlockSpec((d, LANES), lambda i, j: (0, 0)),
                     pl.BlockSpec((1, LANES), lambda i, j: (0, 0))]
        args += [wdt, dtb]
        out_shape.append(jax.ShapeDtypeStruct((t, LANES), F32))
        out_specs.append(pl.BlockSpec((tm, LANES), lambda i, j: (i, 0)))
    res = pl.pallas_call(
        functools.partial(_inproj_kernel, has_dt=has_dt),
        grid=(t // tm, n // tn),
        in_specs=in_specs,
        out_specs=out_specs,
        out_shape=out_shape,
        scratch_shapes=[pltpu.VMEM((tm, d), BF16)],
        compiler_params=_cparams("parallel", "arbitrary"),
        name="inproj_dt" if has_dt else "inproj",
    )(*args)
    return res if has_dt else res[0]


def _conv_kernel(prev_ref, cur_ref, next_ref, w_ref, b_ref, out_ref, *, nl):
    i = pl.program_id(1)
    tl = cur_ref.shape[0]
    prev = jnp.where(i > 0, prev_ref[...], 0.0)
    nxt = jnp.where(i < nl - 1, next_ref[...], 0.0)
    ext = jnp.concatenate([prev, cur_ref[...], nxt], axis=0)
    n = tl + 2 * SUBLANES
    acc = jnp.zeros(cur_ref.shape, F32) + b_ref[...]
    half = SSD_CONV_W // 2
    for k in range(SSD_CONV_W):
        shift = (half - k) % n
        r = ext if shift == 0 else pltpu.roll(ext, shift, 0)
        acc = acc + r[SUBLANES:SUBLANES + tl] * w_ref[k:k + 1, :]
    out_ref[...] = _silu(acc).astype(out_ref.dtype)


def _conv_silu(proj, w, b, nb, seq, col0_in, col0_w, ncols, out_dtype):
    t = proj.shape[0]
    cb = 512
    tl = _tile(seq, 512)
    nl = seq // tl
    rb = tl // SUBLANES
    nrb = t // SUBLANES
    ci, cw = col0_in // cb, col0_w // cb
    return pl.pallas_call(
        functools.partial(_conv_kernel, nl=nl),
        grid=(nb, nl, ncols // cb),
        in_specs=[
            pl.BlockSpec((SUBLANES, cb), lambda bb, i, c: (jnp.maximum((bb * nl + i) * rb - 1, 0), ci + c)),
            pl.BlockSpec((tl, cb), lambda bb, i, c: (bb * nl + i, ci + c)),
            pl.BlockSpec((SUBLANES, cb), lambda bb, i, c: (jnp.minimum((bb * nl + i + 1) * rb, nrb - 1), ci + c)),
            pl.BlockSpec((SSD_CONV_W, cb), lambda bb, i, c: (0, cw + c)),
            pl.BlockSpec((1, cb), lambda bb, i, c: (0, cw + c)),
        ],
        out_specs=pl.BlockSpec((tl, cb), lambda bb, i, c: (bb * nl + i, c)),
        out_shape=jax.ShapeDtypeStruct((t, ncols), out_dtype),
        compiler_params=_cparams("parallel", "parallel", "parallel"),
        name="conv_silu",
    )(proj, proj, proj, w, b)


def _ssd_spread_table(reverse):
    lane0 = SSD_HEADS if reverse else 0
    ee = np.zeros((2 * LANES, SSD_INNER), np.float32)
    for h in range(SSD_HEADS):
        for piece in range(2):
            ee[piece * LANES + lane0 + h, h * SSD_HEAD_DIM:(h + 1) * SSD_HEAD_DIM] = 1.0
    return jnp.asarray(ee, BF16)


def _ssd_kernel(xs_ref, b_ref, c_ref, dt_ref, alog_ref, ee_ref, *rest, reverse):
    q = SSD_CHUNK
    gw = SSD_HPG * SSD_HEAD_DIM
    if reverse:
        yf_ref, dsk_ref, y_ref, state_ref = rest
    else:
        y_ref, state_ref = rest

    @pl.when(pl.program_id(1) == 0)
    def _():
        state_ref[...] = jnp.zeros(state_ref.shape, F32)

    lane0 = SSD_HEADS if reverse else 0
    row = lax.broadcasted_iota(jnp.int32, (q, q), 0)
    col = lax.broadcasted_iota(jnp.int32, (q, q), 1)
    mask = (row <= col) if reverse else (row >= col)
    tri = jnp.where(mask, 1.0, 0.0).astype(F32)
    head_of_lane = lax.broadcasted_iota(jnp.int32, (q, gw), 1) // SSD_HEAD_DIM

    pieces2 = lambda x: jnp.concatenate(_split2(x), axis=1).astype(BF16)

    def chunk(rows):
        dtc = dt_ref[rows]
        a = dtc * (-jnp.exp(alog_ref[...]))
        c = jnp.dot(tri, a, precision=lax.Precision.HIGHEST, preferred_element_type=F32)
        c_t = c.T
        tot = c[0:1, :] if reverse else c[q - 1:q, :]
        ec = jnp.exp(c)
        ed = jnp.exp(tot - c)

        lhs_w = jnp.concatenate([pieces2(dtc), pieces2(dtc * ed), pieces2(ec)], axis=0)
        spread = jnp.dot(lhs_w, ee_ref[...], preferred_element_type=F32)

        for g in range(SSD_GROUPS):
            gs = slice(g * gw, (g + 1) * gw)
            bg = b_ref[rows, g * SSD_STATE:(g + 1) * SSD_STATE]
            cg = c_ref[rows, g * SSD_STATE:(g + 1) * SSD_STATE]
            dt_e, dted_e, ec_e = spread[0:q, gs], spread[q:2 * q, gs], spread[2 * q:3 * q, gs]
            xs_g = xs_ref[rows, gs]
            xdt = (xs_g * dt_e).astype(BF16)
            xw = (xs_g * dted_e).astype(BF16)
            cb = lax.dot_general(cg, bg, (((1,), (1,)), ((), ())), preferred_element_type=F32)
            bg_t = bg.astype(F32).T.astype(BF16)
            st_new = jnp.dot(bg_t, xw, preferred_element_type=F32)
            prev = state_ref[g]
            y_off = jnp.dot(cg, prev.astype(BF16), preferred_element_type=F32) * ec_e
            etot_e = ec_e[0:1, :] if reverse else ec_e[q - 1:q, :]
            state_ref[g] = prev * etot_e + st_new
            ms, xd = [], []
            for r in range(SSD_HPG):
                h = g * SSD_HPG + r
                seg = c[:, lane0 + h:lane0 + h + 1] - c_t[lane0 + h:lane0 + h + 1, :]
                lmat = jnp.exp(jnp.where(mask, seg, -jnp.inf))
                ms.append((cb * lmat).astype(BF16))
                xd.append(jnp.where(head_of_lane == r, xdt, jnp.zeros_like(xdt)))
            y_diag = jnp.dot(jnp.concatenate(ms, axis=1), jnp.concatenate(xd, axis=0),
                             preferred_element_type=F32)
            y_g = y_diag + y_off
            if reverse:
                y_g = yf_ref[rows, gs] + y_g + xs_g * dsk_ref[:, gs]
            y_ref[rows, gs] = y_g

    n_chunks = dt_ref.shape[0] // q
    for ci in (reversed(range(n_chunks)) if reverse else range(n_chunks)):
        chunk(slice(ci * q, (ci + 1) * q))


def _ssd_scan(xs, bc, dt, alog, nb, seq, yf=None, dsk=None):
    reverse = yf is not None
    t = xs.shape[0]
    q = _tile(seq, SSD_CHUNKS_PER_STEP * SSD_CHUNK)
    nc = seq // q
    ee = _ssd_spread_table(reverse)

    def rblk(bb, j):
        return bb * nc + ((nc - 1 - j) if reverse else j)

    row_spec = pl.BlockSpec((q, SSD_INNER), lambda bb, j: (rblk(bb, j), 0))
    in_specs = [
        row_spec,
        pl.BlockSpec((q, SSD_GN), lambda bb, j: (rblk(bb, j), 0)),
        pl.BlockSpec((q, SSD_GN), lambda bb, j: (rblk(bb, j), 1)),
        pl.BlockSpec((q, LANES), lambda bb, j: (rblk(bb, j), 0)),
        pl.BlockSpec((1, LANES), lambda bb, j: (0, 0)),
        pl.BlockSpec(ee.shape, lambda bb, j: (0, 0)),
    ]
    args = [xs, bc, bc, dt, alog, ee]
    if reverse:
        in_specs += [row_spec, pl.BlockSpec((1, SSD_INNER), lambda bb, j: (0, 0))]
        args += [yf, dsk]
    return pl.pallas_call(
        functools.partial(_ssd_kernel, reverse=reverse),
        grid=(nb, nc),
        in_specs=in_specs,
        out_specs=row_spec,
        out_shape=jax.ShapeDtypeStruct((t, SSD_INNER), F32),
        scratch_shapes=[pltpu.VMEM((SSD_GROUPS, SSD_STATE, SSD_HPG * SSD_HEAD_DIM), F32)],
        compiler_params=_cparams("parallel", "arbitrary"),
        name="ssd_scan_bwd" if reverse else "ssd_scan_fwd",
    )(*args)


def _ple_tail(x, mix, p_ref, wg_ref, wp_ref, gng_ref, png_ref, out_ref):
    x1 = x + mix
    gate = jax.nn.sigmoid(
        jnp.dot(_rms(x1, gng_ref[...]).astype(BF16), wg_ref[...], preferred_element_type=F32))
    e = _rms(jnp.dot(p_ref[...].astype(BF16), wp_ref[...], preferred_element_type=F32), png_ref[...])
    out_ref[...] = x1 + gate * e


def _ssd_post_kernel(y_ref, z_ref, x_ref, p_ref, ng_ref, wo_ref,
                     wg_ref, wp_ref, gng_ref, png_ref, out_ref):
    y = _rms(y_ref[...] * _silu(z_ref[...]), ng_ref[...])
    mix = jnp.dot(y.astype(BF16), wo_ref[...], preferred_element_type=F32)
    _ple_tail(x_ref[...], mix, p_ref, wg_ref, wp_ref, gng_ref, png_ref, out_ref)


def _att_post_kernel(o_ref, gate_ref, x_ref, p_ref, wo_ref, wg_ref, wp_ref, gng_ref, png_ref, out_ref):
    u = o_ref[...] * _silu(gate_ref[...])
    mix = jnp.dot(u.astype(BF16), wo_ref[...], preferred_element_type=F32)
    _ple_tail(x_ref[...], mix, p_ref, wg_ref, wp_ref, gng_ref, png_ref, out_ref)


def _row_spec(tm, width, colblk=0):
    return pl.BlockSpec((tm, width), lambda i: (i, colblk))


def _full_spec(shape):
    return pl.BlockSpec(shape, lambda i: (0,) * len(shape))


def _ssd_post(y, proj, x, p, ng, wo, wg, wp, gng, png):
    t = x.shape[0]
    tm = _tile(t, 512)
    return pl.pallas_call(
        _ssd_post_kernel,
        grid=(t // tm,),
        in_specs=[
            _row_spec(tm, SSD_INNER),
            _row_spec(tm, SSD_INNER, 0),
            _row_spec(tm, D_MODEL), _row_spec(tm, PLE_DIM),
            _full_spec((1, SSD_INNER)),
            _full_spec(wo.shape), _full_spec(wg.shape), _full_spec(wp.shape),
            _full_spec((1, D_MODEL)), _full_spec((1, D_MODEL)),
        ],
        out_specs=_row_spec(tm, D_MODEL),
        out_shape=jax.ShapeDtypeStruct((t, D_MODEL), F32),
        compiler_params=_cparams("parallel"),
        name="ssd_post",
    )(y, proj, x, p, ng, wo, wg, wp, gng, png)


def _att_post(o, proj, x, p, wo, wg, wp, gng, png):
    t = x.shape[0]
    tm = _tile(t, 512)
    return pl.pallas_call(
        _att_post_kernel,
        grid=(t // tm,),
        in_specs=[
            _row_spec(tm, D_MODEL),
            _row_spec(tm, D_MODEL, 3),
            _row_spec(tm, D_MODEL), _row_spec(tm, PLE_DIM),
            _full_spec(wo.shape), _full_spec(wg.shape), _full_spec(wp.shape),
            _full_spec((1, D_MODEL)), _full_spec((1, D_MODEL)),
        ],
        out_specs=_row_spec(tm, D_MODEL),
        out_shape=jax.ShapeDtypeStruct((t, D_MODEL), F32),
        compiler_params=_cparams("parallel"),
        name="att_post",
    )(o, proj, x, p, wo, wg, wp, gng, png)


def _attprep_kernel(q_ref, k_ref, v_ref, qg_ref, kg_ref, qn_ref, kt_ref, vb_ref):
    shape = q_ref.shape
    first = lax.broadcasted_iota(jnp.int32, shape, 1) < ATT_QK_DIM

    def norm(x, g):
        x2 = x * x
        s1 = jnp.sum(jnp.where(first, x2, 0.0), axis=-1, keepdims=True)
        s2 = jnp.sum(jnp.where(first, 0.0, x2), axis=-1, keepdims=True)
        ms = jnp.where(first, s1, s2) * (1.0 / ATT_QK_DIM)
        return x * lax.rsqrt(ms + EPS) * g

    qn_ref[0] = (norm(q_ref[...], qg_ref[...]) * (ATT_QK_DIM ** -0.5)).astype(BF16)
    kt_ref[0] = norm(k_ref[...], kg_ref[...]).T.astype(BF16)
    vb_ref[0] = v_ref[...].astype(BF16)


def _att_prep(proj, qg, kg, nb, seq):
    tl = _tile(seq, 512)
    nl = seq // tl
    nh = ATT_HEADS
    blk = lambda off: pl.BlockSpec((tl, LANES), lambda bb, h, i: (bb * nl + i, off + h))
    return pl.pallas_call(
        _attprep_kernel,
        grid=(nb, nh, nl),
        in_specs=[blk(0), blk(nh), blk(2 * nh), _g3((1, LANES)), _g3((1, LANES))],
        out_specs=[
            pl.BlockSpec((1, tl, LANES), lambda bb, h, i: (bb * nh + h, i, 0)),
            pl.BlockSpec((1, LANES, tl), lambda bb, h, i: (bb * nh + h, 0, i)),
            pl.BlockSpec((1, tl, LANES), lambda bb, h, i: (bb * nh + h, i, 0)),
        ],
        out_shape=[
            jax.ShapeDtypeStruct((nb * nh, seq, LANES), BF16),
            jax.ShapeDtypeStruct((nb * nh, LANES, seq), BF16),
            jax.ShapeDtypeStruct((nb * nh, seq, LANES), BF16),
        ],
        compiler_params=_cparams("parallel", "parallel", "parallel"),
        name="att_prep",
    )(proj, proj, proj, qg, kg)


def _g3(shape):
    return pl.BlockSpec(shape, lambda bb, h, i: (0,) * len(shape))


def _flash_kernel(q_ref, kt_ref, v_ref, slope_ref, lam_ref, sg_ref, o_ref, *, tk, nk, lambda_init):
    tq = q_ref.shape[1]
    q = q_ref[0]
    first = lax.broadcasted_iota(jnp.int32, q.shape, 1) < ATT_QK_DIM
    zero = jnp.zeros_like(q)
    q1 = jnp.where(first, q, zero)
    q2 = jnp.where(first, zero, q)
    slope = slope_ref[0][:, 0:1]
    rel = (lax.broadcasted_iota(jnp.int32, (tq, tk), 1)
           - lax.broadcasted_iota(jnp.int32, (tq, tk), 0))
    q0 = pl.program_id(2) * tq

    def update(qm, kt, vt, bias, m, l, acc):
        s = jnp.dot(qm, kt, preferred_element_type=F32) - bias
        m_new = jnp.maximum(m, jnp.max(s, axis=-1, keepdims=True))
        alpha = jnp.exp(m - m_new)
        p = jnp.exp(s - m_new)
        l = alpha * l + jnp.sum(p, axis=-1, keepdims=True)
        acc = alpha * acc + jnp.dot(p.astype(BF16), vt, preferred_element_type=F32)
        return m_new, l, acc

    def body(kidx, carry):
        m1, l1, a1, m2, l2, a2 = carry
        k0 = pl.multiple_of(kidx * tk, tk)
        kt = kt_ref[0, :, pl.ds(k0, tk)]
        vt = v_ref[0, pl.ds(k0, tk), :]
        bias = jnp.abs(rel + (k0 - q0)).astype(F32) * slope
        m1, l1, a1 = update(q1, kt, vt, bias, m1, l1, a1)
        m2, l2, a2 = update(q2, kt, vt, bias, m2, l2, a2)
        return m1, l1, a1, m2, l2, a2

    neg = jnp.full((tq, 1), -jnp.inf, F32)
    zl = jnp.zeros((tq, 1), F32)
    za = jnp.zeros((tq, ATT_V_DIM), F32)
    m1, l1, a1, m2, l2, a2 = lax.fori_loop(0, nk, body, (neg, zl, za, neg, zl, za))

    lv = lam_ref[...]
    lam = (jnp.exp(jnp.sum(lv[0:1] * lv[1:2], axis=-1, keepdims=True))
           - jnp.exp(jnp.sum(lv[2:3] * lv[3:4], axis=-1, keepdims=True)) + lambda_init)
    o = a1 / l1 - lam * (a2 / l2)
    o_ref[...] = _rms(o, sg_ref[...]) * (1.0 - lambda_init)


def _flash(qn, kt, vb, slopes, lamv, sg, nb, seq, lambda_init):
    nh = ATT_HEADS
    tq = _tile(seq, 512)
    tk = _tile(seq, 512)
    nq = seq // tq
    return pl.pallas_call(
        functools.partial(_flash_kernel, tk=tk, nk=seq // tk, lambda_init=lambda_init),
        grid=(nb, nh, nq),
        in_specs=[
            pl.BlockSpec((1, tq, LANES), lambda bb, h, i: (bb * nh + h, i, 0)),
            pl.BlockSpec((1, LANES, seq), lambda bb, h, i: (bb * nh + h, 0, 0)),
            pl.BlockSpec((1, seq, LANES), lambda bb, h, i: (bb * nh + h, 0, 0)),
            pl.BlockSpec((1, 1, LANES), lambda bb, h, i: (h, 0, 0)),
            _g3((4, ATT_QK_DIM)),
            _g3((1, ATT_V_DIM)),
        ],
        out_specs=pl.BlockSpec((tq, LANES), lambda bb, h, i: (bb * nq + i, h)),
        out_shape=jax.ShapeDtypeStruct((nb * seq, nh * ATT_V_DIM), F32),
        compiler_params=_cparams("parallel", "parallel", "arbitrary"),
        name="flash_diff_attn",
    )(qn, kt, vb, slopes, lamv, sg)


FLASH_TILE = 512
POS_U = 0
POS_W_LOW = 2
POS_OFF = 4
POS_SHIFT = 7
POS_W_UP = 8
BAND_EXP_CUTOFF = 112.0
SHIFT_SAFE_MAX = 40.0


def _split2(x):
    hi = x.astype(BF16).astype(F32)
    lo = (x - hi).astype(BF16).astype(F32)
    return hi, lo


def _split3(x):
    hi = x.astype(BF16).astype(F32)
    r = x - hi
    mid = r.astype(BF16).astype(F32)
    lo = (r - mid).astype(BF16).astype(F32)
    return hi, mid, lo


def _attprep_band_kernel(q_ref, k_ref, v_ref, qg_ref, kg_ref, slope_ref, qn_ref, kt_ref, vb_ref):
    tl = q_ref.shape[0]
    gr = lax.broadcasted_iota(jnp.int32, (3 * LANES, LANES), 0) % LANES
    gc = lax.broadcasted_iota(jnp.int32, (3 * LANES, LANES), 1)
    same_half = jnp.where((gr < ATT_QK_DIM) == (gc < ATT_QK_DIM), 1.0, 0.0).astype(BF16)

    def norm(x, g):
        pieces = jnp.concatenate(_split3(x * x), axis=1).astype(BF16)
        ms = jnp.dot(pieces, same_half, preferred_element_type=F32) * (1.0 / ATT_QK_DIM)
        return x * lax.rsqrt(ms + EPS) * g

    mq = jnp.max(jnp.abs(qg_ref[...]), axis=-1, keepdims=True)
    mk = jnp.max(jnp.abs(kg_ref[...]), axis=-1, keepdims=True)
    shift = mq * mk * (ATT_QK_DIM ** 0.5 * 1.02)

    r = lax.broadcasted_iota(jnp.int32, (LANES, tl), 0)
    j = lax.broadcasted_iota(jnp.int32, (LANES, tl), 1).astype(F32)
    minus_one = (r == POS_U) | (r == POS_U + 1) | ((r >= POS_OFF) & (r < POS_OFF + 3))
    pos_const = jnp.where(r == POS_SHIFT, -shift, jnp.where(minus_one, -1.0, 0.0))
    ones = jnp.ones((tl, LANES), BF16)

    for h in range(ATT_HEADS):
        hs = slice(h * LANES, (h + 1) * LANES)
        qn_ref[h] = (norm(q_ref[:, hs], qg_ref[...]) * (ATT_QK_DIM ** -0.5)).astype(BF16)
        kt_ref[h, 0:LANES, :] = norm(k_ref[:, hs], kg_ref[...]).T.astype(BF16)
        slope = slope_ref[h][:, 0:1]
        wl_hi, wl_lo = _split2(slope * (tl - 1.0 - j))
        wu_hi, wu_lo = _split2(slope * j)
        pos = jnp.where(r == POS_W_LOW, -wl_hi, pos_const)
        pos = jnp.where(r == POS_W_LOW + 1, -wl_lo, pos)
        pos = jnp.where(r == POS_W_UP, -wu_hi, pos)
        pos = jnp.where(r == POS_W_UP + 1, -wu_lo, pos)
        kt_ref[h, LANES:2 * LANES, :] = pos.astype(BF16)
        vb_ref[h, :, 0:LANES] = v_ref[:, hs].astype(BF16)
        vb_ref[h, :, LANES:2 * LANES] = ones


def _att_prep_band(proj, qg, kg, slopes, nb, seq):
    tl = _tile(seq, FLASH_TILE)
    nl = seq // tl
    nh = ATT_HEADS
    w = nh * LANES
    blk = lambda c: pl.BlockSpec((tl, w), lambda bb, i: (bb * nl + i, c))
    full = lambda shape: pl.BlockSpec(shape, lambda bb, i: (0,) * len(shape))
    return pl.pallas_call(
        _attprep_band_kernel,
        grid=(nb, nl),
        in_specs=[blk(0), blk(1), blk(2), full((1, LANES)), full((1, LANES)),
                  full((nh, 1, LANES))],
        out_specs=[
            pl.BlockSpec((nh, tl, LANES), lambda bb, i: (bb, i, 0)),
            pl.BlockSpec((nh, 2 * LANES, tl), lambda bb, i: (bb, 0, i)),
            pl.BlockSpec((nh, tl, 2 * LANES), lambda bb, i: (bb, i, 0)),
        ],
        out_shape=[
            jax.ShapeDtypeStruct((nb * nh, seq, LANES), BF16),
            jax.ShapeDtypeStruct((nb * nh, 2 * LANES, seq), BF16),
            jax.ShapeDtypeStruct((nb * nh, seq, 2 * LANES), BF16),
        ],
        compiler_params=_cparams("parallel", "parallel"),
        name="att_prep_band",
    )(proj, proj, proj, qg, kg, slopes)


def _flash_band_kernel(q_ref, kt_ref, v_ref, slope_ref, lam_ref, sg_ref, o_ref,
                       lhs_ref, acc_ref, p_ref, base_ref, bias_ref, *, tk, nk, lambda_init):
    tq = q_ref.shape[1]
    assert tq == tk
    h = pl.program_id(1)
    qi = pl.program_id(2)
    q0 = qi * tq
    slope_row = slope_ref[0]
    lane = lax.broadcasted_iota(jnp.int32, (tq, LANES), 1)
    rowf = lax.broadcasted_iota(jnp.int32, (tq, LANES), 0).astype(F32)

    q = q_ref[0]
    zero = jnp.zeros_like(q)
    first = lane < ATT_QK_DIM
    lhs_ref[0:tq] = jnp.where(first, q, zero)
    lhs_ref[tq:2 * tq] = jnp.where(first, zero, q)
    acc_ref[...] = jnp.zeros(acc_ref.shape, F32)

    def pos_base(u, w_lane):
        u_hi, u_lo = _split2(u)
        ones = (lane == POS_SHIFT) | (lane == w_lane) | (lane == w_lane + 1)
        base = jnp.where(ones, 1.0, 0.0)
        base = jnp.where(lane == POS_U, u_hi, base)
        return jnp.where(lane == POS_U + 1, u_lo, base)

    off_lanes = (lane >= POS_OFF) & (lane < POS_OFF + 3)
    lane1 = lax.broadcasted_iota(jnp.int32, (2 * SUBLANES, LANES), 1)
    slope_tile = jnp.broadcast_to(slope_row, (2 * SUBLANES, LANES))

    def pos_half(base, dist):
        if dist is not None:
            hi, mid, lo = _split3(slope_tile * dist.astype(F32))
            orow = jnp.where(lane1 == POS_OFF, hi, jnp.where(lane1 == POS_OFF + 1, mid, lo))
            base = jnp.where(off_lanes, orow[0:1, :], base)
        return base.astype(BF16)

    halves = (slice(0, tq), slice(tq, 2 * tq))

    def raw_scores(kidx, pos):
        k0 = pl.multiple_of(kidx * tk, tk)
        lhs = jnp.concatenate([jnp.concatenate([lhs_ref[rows], pos], axis=1) for rows in halves],
                              axis=0)
        return jnp.dot(lhs, kt_ref[0, :, pl.ds(k0, tk)], preferred_element_type=F32)

    def weighted_sum(kidx, slot):
        k0 = pl.multiple_of(kidx * tk, tk)
        acc_ref[...] += jnp.dot(p_ref[slot], v_ref[0, pl.ds(k0, tk), :], preferred_element_type=F32)


    band = lax.shift_left(jnp.int32(int(BAND_EXP_CUTOFF)), h + 1)
    k_lo = jnp.maximum(q0 + 1 - band, 0) // tk
    k_hi = jnp.minimum((q0 + tq - 2 + band + tk) // tk, nk)
    n_low = qi - k_lo
    n_off_diag = n_low + (k_hi - qi - 1)

    @pl.when(qi == 0)
    def _():
        base_ref[0] = pos_base(slope_row * rowf, POS_W_LOW)
        base_ref[1] = pos_base(slope_row * (tq - 1.0 - rowf), POS_W_UP)
        rel = (lax.broadcasted_iota(jnp.int32, (tq, tk), 1)
               - lax.broadcasted_iota(jnp.int32, (tq, tk), 0))
        bias_ref[...] = jnp.abs(rel).astype(F32) * slope_row[:, 0:1]

    s = raw_scores(qi, pos_half(jnp.where(lane == POS_SHIFT, 1.0, 0.0), None))
    for rows in halves:
        p_ref[0, rows] = jnp.exp(s[rows] - bias_ref[...]).astype(BF16)

    def tile_of(u):
        return jnp.where(u < n_low, k_lo + u, qi + 1 + u - n_low)

    def step(u, prev, s_in, s_out):
        below = u < n_low
        kidx = tile_of(u)
        dist = jnp.where(below, q0 - kidx * tk - (tk - 1), kidx * tk - q0 - (tq - 1))
        weighted_sum(prev, s_in)
        s = raw_scores(kidx, pos_half(base_ref[jnp.where(below, 0, 1)], dist))
        p_ref[s_out] = jnp.exp(s).astype(BF16)
        return kidx

    def quad_body(i, prev):
        u = 4 * i
        return step(u + 3, step(u + 2, step(u + 1, step(u, prev, 0, 1), 1, 2), 2, 3), 3, 0)

    n_quads = n_off_diag // 4
    prev = lax.fori_loop(0, n_quads, quad_body, qi)
    done = 4 * n_quads
    n_pairs = (n_off_diag - done) // 2
    prev = lax.fori_loop(0, n_pairs, lambda i, p: step(done + 1, step(done, p, 0, 1), 1, 0), prev)
    odd = n_off_diag - done - 2 * n_pairs

    @pl.when(odd == 1)
    def _():
        weighted_sum(step(n_off_diag - 1, prev, 0, 1), 1)

    @pl.when(odd == 0)
    def _():
        weighted_sum(prev, 0)

    lv = lam_ref[...]
    lam = (jnp.exp(jnp.sum(lv[0:1] * lv[1:2], axis=-1, keepdims=True))
           - jnp.exp(jnp.sum(lv[2:3] * lv[3:4], axis=-1, keepdims=True)) + lambda_init)
    a1 = acc_ref[0:tq, 0:LANES]
    l1 = acc_ref[0:tq, LANES:2 * LANES]
    a2 = acc_ref[tq:2 * tq, 0:LANES]
    l2 = acc_ref[tq:2 * tq, LANES:2 * LANES]
    o = a1 / l1 - lam * (a2 / l2)
    o_ref[...] = _rms(o, sg_ref[...]) * (1.0 - lambda_init)


def _flash_band(qn, kt, vb, slopes, lamv, sg, nb, seq, lambda_init):
    nh = ATT_HEADS
    tq = _tile(seq, FLASH_TILE)
    nq = seq // tq
    return pl.pallas_call(
        functools.partial(_flash_band_kernel, tk=tq, nk=nq, lambda_init=lambda_init),
        grid=(nb, nh, nq),
        in_specs=[
            pl.BlockSpec((1, tq, LANES), lambda bb, h, i: (bb * nh + h, i, 0)),
            pl.BlockSpec((1, 2 * LANES, seq), lambda bb, h, i: (bb * nh + h, 0, 0)),
            pl.BlockSpec((1, seq, 2 * LANES), lambda bb, h, i: (bb * nh + h, 0, 0)),
            pl.BlockSpec((1, 1, LANES), lambda bb, h, i: (h, 0, 0)),
            _g3((4, ATT_QK_DIM)),
            _g3((1, ATT_V_DIM)),
        ],
        out_specs=pl.BlockSpec((tq, LANES), lambda bb, h, i: (bb * nq + i, h)),
        out_shape=jax.ShapeDtypeStruct((nb * seq, nh * ATT_V_DIM), F32),
        scratch_shapes=[
            pltpu.VMEM((2 * tq, LANES), BF16),
            pltpu.VMEM((2 * tq, 2 * LANES), F32),
            pltpu.VMEM((4, 2 * tq, tq), BF16),
            pltpu.VMEM((2, tq, LANES), F32),
            pltpu.VMEM((tq, tq), F32),
        ],
        compiler_params=_cparams("parallel", "parallel", "arbitrary"),
        name="flash_band",
    )(qn, kt, vb, slopes, lamv, sg)


def _attention(proj, lw, slopes, nb, seq, lambda_init):
    def banded(proj):
        qn, kt, vb = _att_prep_band(proj, lw["qg"], lw["kg"], slopes, nb, seq)
        return _flash_band(qn, kt, vb, slopes, lw["lam"], lw["sub_g"], nb, seq, lambda_init)

    def online(proj):
        qn, kt, vb = _att_prep(proj, lw["qg"], lw["kg"], nb, seq)
        return _flash(qn, kt, vb, slopes, lw["lam"], lw["sub_g"], nb, seq, lambda_init)

    shift = jnp.max(jnp.abs(lw["qg"])) * jnp.max(jnp.abs(lw["kg"])) * (ATT_QK_DIM ** 0.5 * 1.02)
    return lax.cond(shift < SHIFT_SAFE_MAX, banded, online, proj)


def _alibi_slopes():
    s = np.array([2.0 ** (-8.0 * (i + 1) / ATT_HEADS) for i in range(ATT_HEADS)], dtype=np.float32)
    return jnp.asarray(np.broadcast_to(s[:, None, None], (ATT_HEADS, 1, LANES)).copy())


def _prep_weights(w):
    row = lambda v: v.reshape(1, -1).astype(F32)
    pad_lanes = lambda v: jnp.pad(v.reshape(1, -1).astype(F32), ((0, 0), (0, LANES - v.size)))
    out = {"layers": [], "slopes": _alibi_slopes()}
    for i in range(DEPTH):
        j = i // 2
        lw = {
            "pre_g": row(w["pre_norm_g"][i]),
            "wg": w["ple_w_gate"][i].astype(BF16),
            "wp": w["ple_w_proj"][i].astype(BF16),
            "gng": row(w["ple_gate_norm_g"][i]),
            "png": row(w["ple_norm_g"][i]),
        }
        if i % 2 == 0:
            win = w["ssd_w_in"][j]
            lw.update(
                w_main=win[:, :SSD_MAIN_COLS].astype(BF16),
                w_dt=jnp.pad(win[:, SSD_MAIN_COLS:], ((0, 0), (0, LANES - 2 * SSD_HEADS))).astype(BF16),
                dt_bias=pad_lanes(w["ssd_dt_bias"][j]),
                a_log=pad_lanes(w["ssd_a_log"][j]),
                conv_w=w["ssd_conv_w"][j].astype(F32),
                conv_b=row(w["ssd_conv_b"][j]),
                d_skip=row(jnp.repeat(w["ssd_d_skip"][j], SSD_HEAD_DIM)),
                norm_g=row(w["ssd_norm_g"][j]),
                w_out=w["ssd_w_out"][j].astype(BF16),
            )
        else:
            lw.update(
                w_in=w["att_w_in"][j].astype(BF16),
                qg=row(jnp.tile(w["att_q_norm_g"][j], 2)),
                kg=row(jnp.tile(w["att_k_norm_g"][j], 2)),
                lam=jnp.stack([w["att_lam_q1"][j], w["att_lam_k1"][j],
                               w["att_lam_q2"][j], w["att_lam_k2"][j]]).astype(F32),
                sub_g=row(w["att_sub_norm_g"][j]),
                w_out=w["att_w_out"][j].astype(BF16),
            )
        out["layers"].append(lw)
    return out


def _trunk(x3, p4, pw):
    nb, seq, d = x3.shape
    t = nb * seq
    x = x3.reshape(t, d)
    for i in range(DEPTH):
        lw = pw["layers"][i]
        p = p4[i].reshape(t, PLE_DIM)
        if i % 2 == 0:
            proj, dt = _inproj(x, lw["pre_g"], lw["w_main"], lw["w_dt"], lw["dt_bias"])
            xs = _conv_silu(proj, lw["conv_w"], lw["conv_b"], nb, seq,
                            SSD_INNER, 0, SSD_INNER, F32)
            bc = _conv_silu(proj, lw["conv_w"], lw["conv_b"], nb, seq,
                            2 * SSD_INNER, SSD_INNER, 2 * SSD_GN, BF16)
            yf = _ssd_scan(xs, bc, dt, lw["a_log"], nb, seq)
            y = _ssd_scan(xs, bc, dt, lw["a_log"], nb, seq, yf, lw["d_skip"])
            x = _ssd_post(y, proj, x, p, lw["norm_g"], lw["w_out"],
                          lw["wg"], lw["wp"], lw["gng"], lw["png"])
        else:
            lambda_init = 0.8 - 0.6 * math.exp(-0.3 * i)
            proj = _inproj(x, lw["pre_g"], lw["w_in"])
            o = _attention(proj, lw, pw["slopes"], nb, seq, lambda_init)
            x = _att_post(o, proj, x, p, lw["w_out"], lw["wg"], lw["wp"], lw["gng"], lw["png"])
    return x.reshape(nb, seq, d)


def kernel(x_prompt, x_sample, p_prompt, p_sample, pre_norm_g, ssd_w_in, ssd_conv_w, ssd_conv_b, ssd_dt_bias, ssd_a_log, ssd_d_skip, ssd_norm_g, ssd_w_out, att_w_in, att_q_norm_g, att_k_norm_g, att_lam_q1, att_lam_k1, att_lam_q2, att_lam_k2, att_sub_norm_g, att_w_out, ple_w_proj, ple_norm_g, ple_gate_norm_g, ple_w_gate):
    pw = _prep_weights(dict(
        pre_norm_g=pre_norm_g, ssd_w_in=ssd_w_in, ssd_conv_w=ssd_conv_w, ssd_conv_b=ssd_conv_b,
        ssd_dt_bias=ssd_dt_bias, ssd_a_log=ssd_a_log, ssd_d_skip=ssd_d_skip, ssd_norm_g=ssd_norm_g,
        ssd_w_out=ssd_w_out, att_w_in=att_w_in, att_q_norm_g=att_q_norm_g, att_k_norm_g=att_k_norm_g,
        att_lam_q1=att_lam_q1, att_lam_k1=att_lam_k1, att_lam_q2=att_lam_q2, att_lam_k2=att_lam_k2,
        att_sub_norm_g=att_sub_norm_g, att_w_out=att_w_out, ple_w_proj=ple_w_proj,
        ple_norm_g=ple_norm_g, ple_gate_norm_g=ple_gate_norm_g, ple_w_gate=ple_w_gate))
    return (_trunk(x_prompt, p_prompt, pw), _trunk(x_sample, p_sample, pw))
```

```python
import functools
import math

import numpy as np
import jax
import jax.numpy as jnp
from jax import lax
from jax.experimental import pallas as pl
from jax.experimental.pallas import tpu as pltpu

F32 = jnp.float32
BF16 = jnp.bfloat16

EPS = 1e-6
D_MODEL = 1024
DEPTH = 4
PLE_DIM = 256

SSD_INNER = 2048
SSD_HEAD_DIM = 64
SSD_HEADS = 32
SSD_GROUPS = 8
SSD_HPG = 4
SSD_STATE = 128
SSD_GN = SSD_GROUPS * SSD_STATE
SSD_CONV_W = 5
SSD_CHUNK = 128
SSD_CHUNKS_PER_STEP = 4
SSD_MAIN_COLS = SSD_INNER + SSD_INNER + 2 * SSD_GN

ATT_HEADS = 8
ATT_QK_DIM = 64
ATT_V_DIM = 128
ATT_IN_COLS = 4096

LANES = 128
SUBLANES = 8
VMEM_LIMIT_BYTES = 56 * 1024 * 1024


def _cparams(*sem):
    return pltpu.CompilerParams(dimension_semantics=sem, vmem_limit_bytes=VMEM_LIMIT_BYTES)


def _tile(n, pref):
    t = min(n, pref)
    assert n % t == 0, (n, pref)
    return t


def _rms(x, g):
    return x * lax.rsqrt(jnp.mean(x * x, axis=-1, keepdims=True) + EPS) * g


def _silu(x):
    return x * jax.nn.sigmoid(x)


def _inproj_kernel(x_ref, g_ref, w_ref, *rest, has_dt):
    if has_dt:
        wdt_ref, dtb_ref, out_ref, dt_ref, hn_ref = rest
    else:
        out_ref, hn_ref = rest

    @pl.when(pl.program_id(1) == 0)
    def _():
        hn_ref[...] = _rms(x_ref[...], g_ref[...]).astype(BF16)
        if has_dt:
            raw = jnp.dot(hn_ref[...], wdt_ref[...], preferred_element_type=F32) + dtb_ref[...]
            dt_ref[...] = jax.nn.softplus(raw)

    out_ref[...] = jnp.dot(hn_ref[...], w_ref[...], preferred_element_type=F32)


def _inproj(x, g, w, wdt=None, dtb=None):
    t, d = x.shape
    n = w.shape[1]
    tm = _tile(t, 1024)
    tn = _tile(n, 2048)
    has_dt = wdt is not None
    in_specs = [
        pl.BlockSpec((tm, d), lambda i, j: (i, 0)),
        pl.BlockSpec((1, d), lambda i, j: (0, 0)),
        pl.BlockSpec((d, tn), lambda i, j: (0, j)),
    ]
    args = [x, g, w]
    out_shape = [jax.ShapeDtypeStruct((t, n), F32)]
    out_specs = [pl.BlockSpec((tm, tn), lambda i, j: (i, j))]
    if has_dt:
        in_specs += [pl.BlockSpec((d, LANES), lambda i, j: (0, 0)),
                     pl.BlockSpec((1, LANES), lambda i, j: (0, 0))]
        args += [wdt, dtb]
        out_shape.append(jax.ShapeDtypeStruct((t, LANES), F32))
        out_specs.append(pl.BlockSpec((tm, LANES), lambda i, j: (i, 0)))
    res = pl.pallas_call(
        functools.partial(_inproj_kernel, has_dt=has_dt),
        grid=(t // tm, n // tn),
        in_specs=in_specs,
        out_specs=out_specs,
        out_shape=out_shape,
        scratch_shapes=[pltpu.VMEM((tm, d), BF16)],
        compiler_params=_cparams("parallel", "arbitrary"),
        name="inproj_dt" if has_dt else "inproj",
    )(*args)
    return res if has_dt else res[0]


def _conv_kernel(prev_ref, cur_ref, next_ref, w_ref, b_ref, out_ref, *, nl):
    i = pl.program_id(1)
    tl = cur_ref.shape[0]
    prev = jnp.where(i > 0, prev_ref[...], 0.0)
    nxt = jnp.where(i < nl - 1, next_ref[...], 0.0)
    ext = jnp.concatenate([prev, cur_ref[...], nxt], axis=0)
    n = tl + 2 * SUBLANES
    acc = jnp.zeros(cur_ref.shape, F32) + b_ref[...]
    half = SSD_CONV_W // 2
    for k in range(SSD_CONV_W):
        shift = (half - k) % n
        r = ext if shift == 0 else pltpu.roll(ext, shift, 0)
        acc = acc + r[SUBLANES:SUBLANES + tl] * w_ref[k:k + 1, :]
    out_ref[...] = _silu(acc).astype(out_ref.dtype)


def _conv_silu(proj, w, b, nb, seq, col0_in, col0_w, ncols, out_dtype):
    t = proj.shape[0]
    cb = 512
    tl = _tile(seq, 512)
    nl = seq // tl
    rb = tl // SUBLANES
    nrb = t // SUBLANES
    ci, cw = col0_in // cb, col0_w // cb
    return pl.pallas_call(
        functools.partial(_conv_kernel, nl=nl),
        grid=(nb, nl, ncols // cb),
        in_specs=[
            pl.BlockSpec((SUBLANES, cb), lambda bb, i, c: (jnp.maximum((bb * nl + i) * rb - 1, 0), ci + c)),
            pl.BlockSpec((tl, cb), lambda bb, i, c: (bb * nl + i, ci + c)),
            pl.BlockSpec((SUBLANES, cb), lambda bb, i, c: (jnp.minimum((bb * nl + i + 1) * rb, nrb - 1), ci + c)),
            pl.BlockSpec((SSD_CONV_W, cb), lambda bb, i, c: (0, cw + c)),
            pl.BlockSpec((1, cb), lambda bb, i, c: (0, cw + c)),
        ],
        out_specs=pl.BlockSpec((tl, cb), lambda bb, i, c: (bb * nl + i, c)),
        out_shape=jax.ShapeDtypeStruct((t, ncols), out_dtype),
        compiler_params=_cparams("parallel", "parallel", "parallel"),
        name="conv_silu",
    )(proj, proj, proj, w, b)


def _ssd_spread_table(reverse):
    lane0 = SSD_HEADS if reverse else 0
    ee = np.zeros((2 * LANES, SSD_INNER), np.float32)
    for h in range(SSD_HEADS):
        for piece in range(2):
            ee[piece * LANES + lane0 + h, h * SSD_HEAD_DIM:(h + 1) * SSD_HEAD_DIM] = 1.0
    return jnp.asarray(ee, BF16)


def _ssd_kernel(xs_ref, b_ref, c_ref, dt_ref, alog_ref, ee_ref, *rest, reverse):
    q = SSD_CHUNK
    gw = SSD_HPG * SSD_HEAD_DIM
    if reverse:
        yf_ref, dsk_ref, y_ref, state_ref = rest
    else:
        y_ref, state_ref = rest

    @pl.when(pl.program_id(1) == 0)
    def _():
        state_ref[...] = jnp.zeros(state_ref.shape, F32)

    lane0 = SSD_HEADS if reverse else 0
    row = lax.broadcasted_iota(jnp.int32, (q, q), 0)
    col = lax.broadcasted_iota(jnp.int32, (q, q), 1)
    mask = (row <= col) if reverse else (row >= col)
    tri = jnp.where(mask, 1.0, 0.0).astype(F32)
    head_of_lane = lax.broadcasted_iota(jnp.int32, (q, gw), 1) // SSD_HEAD_DIM

    pieces2 = lambda x: jnp.concatenate(_split2(x), axis=1).astype(BF16)

    def chunk(rows):
        dtc = dt_ref[rows]
        a = dtc * (-jnp.exp(alog_ref[...]))
        c = jnp.dot(tri, a, precision=lax.Precision.HIGHEST, preferred_element_type=F32)
        c_t = c.T
        tot = c[0:1, :] if reverse else c[q - 1:q, :]
        ec = jnp.exp(c)
        ed = jnp.exp(tot - c)

        lhs_w = jnp.concatenate([pieces2(dtc), pieces2(dtc * ed), pieces2(ec)], axis=0)
        spread = jnp.dot(lhs_w, ee_ref[...], preferred_element_type=F32)

        for g in range(SSD_GROUPS):
            gs = slice(g * gw, (g + 1) * gw)
            bg = b_ref[rows, g * SSD_STATE:(g + 1) * SSD_STATE]
            cg = c_ref[rows, g * SSD_STATE:(g + 1) * SSD_STATE]
            dt_e, dted_e, ec_e = spread[0:q, gs], spread[q:2 * q, gs], spread[2 * q:3 * q, gs]
            xs_g = xs_ref[rows, gs]
            xdt = (xs_g * dt_e).astype(BF16)
            xw = (xs_g * dted_e).astype(BF16)
            cb = lax.dot_general(cg, bg, (((1,), (1,)), ((), ())), preferred_element_type=F32)
            bg_t = bg.astype(F32).T.astype(BF16)
            st_new = jnp.dot(bg_t, xw, preferred_element_type=F32)
            prev = state_ref[g]
            y_off = jnp.dot(cg, prev.astype(BF16), preferred_element_type=F32) * ec_e
            etot_e = ec_e[0:1, :] if reverse else ec_e[q - 1:q, :]
            state_ref[g] = prev * etot_e + st_new
            ms, xd = [], []
            for r in range(SSD_HPG):
                h = g * SSD_HPG + r
                seg = c[:, lane0 + h:lane0 + h + 1] - c_t[lane0 + h:lane0 + h + 1, :]
                lmat = jnp.exp(jnp.where(mask, seg, -jnp.inf))
                ms.append((cb * lmat).astype(BF16))
                xd.append(jnp.where(head_of_lane == r, xdt, jnp.zeros_like(xdt)))
            y_diag = jnp.dot(jnp.concatenate(ms, axis=1), jnp.concatenate(xd, axis=0),
                             preferred_element_type=F32)
            y_g = y_diag + y_off
            if reverse:
                y_g = yf_ref[rows, gs] + y_g + xs_g * dsk_ref[:, gs]
            y_ref[rows, gs] = y_g

    n_chunks = dt_ref.shape[0] // q
    for ci in (reversed(range(n_chunks)) if reverse else range(n_chunks)):
        chunk(slice(ci * q, (ci + 1) * q))


def _ssd_scan(xs, bc, dt, alog, nb, seq, yf=None, dsk=None):
    reverse = yf is not None
    t = xs.shape[0]
    q = _tile(seq, SSD_CHUNKS_PER_STEP * SSD_CHUNK)
    nc = seq // q
    ee = _ssd_spread_table(reverse)

    def rblk(bb, j):
        return bb * nc + ((nc - 1 - j) if reverse else j)

    row_spec = pl.BlockSpec((q, SSD_INNER), lambda bb, j: (rblk(bb, j), 0))
    in_specs = [
        row_spec,
        pl.BlockSpec((q, SSD_GN), lambda bb, j: (rblk(bb, j), 0)),
        pl.BlockSpec((q, SSD_GN), lambda bb, j: (rblk(bb, j), 1)),
        pl.BlockSpec((q, LANES), lambda bb, j: (rblk(bb, j), 0)),
        pl.BlockSpec((1, LANES), lambda bb, j: (0, 0)),
        pl.BlockSpec(ee.shape, lambda bb, j: (0, 0)),
    ]
    args = [xs, bc, bc, dt, alog, ee]
    if reverse:
        in_specs += [row_spec, pl.BlockSpec((1, SSD_INNER), lambda bb, j: (0, 0))]
        args += [yf, dsk]
    return pl.pallas_call(
        functools.partial(_ssd_kernel, reverse=reverse),
        grid=(nb, nc),
        in_specs=in_specs,
        out_specs=row_spec,
        out_shape=jax.ShapeDtypeStruct((t, SSD_INNER), F32),
        scratch_shapes=[pltpu.VMEM((SSD_GROUPS, SSD_STATE, SSD_HPG * SSD_HEAD_DIM), F32)],
        compiler_params=_cparams("parallel", "arbitrary"),
        name="ssd_scan_bwd" if reverse else "ssd_scan_fwd",
    )(*args)


def _ple_tail(x, mix, p_ref, wg_ref, wp_ref, gng_ref, png_ref, out_ref):
    x1 = x + mix
    gate = jax.nn.sigmoid(
        jnp.dot(_rms(x1, gng_ref[...]).astype(BF16), wg_ref[...], preferred_element_type=F32))
    e = _rms(jnp.dot(p_ref[...].astype(BF16), wp_ref[...], preferred_element_type=F32), png_ref[...])
    out_ref[...] = x1 + gate * e


def _ssd_post_kernel(y_ref, z_ref, x_ref, p_ref, ng_ref, wo_ref,
                     wg_ref, wp_ref, gng_ref, png_ref, out_ref):
    y = _rms(y_ref[...] * _silu(z_ref[...]), ng_ref[...])
    mix = jnp.dot(y.astype(BF16), wo_ref[...], preferred_element_type=F32)
    _ple_tail(x_ref[...], mix, p_ref, wg_ref, wp_ref, gng_ref, png_ref, out_ref)


def _att_post_kernel(o_ref, gate_ref, x_ref, p_ref, wo_ref, wg_ref, wp_ref, gng_ref, png_ref, out_ref):
    u = o_ref[...] * _silu(gate_ref[...])
    mix = jnp.dot(u.astype(BF16), wo_ref[...], preferred_element_type=F32)
    _ple_tail(x_ref[...], mix, p_ref, wg_ref, wp_ref, gng_ref, png_ref, out_ref)


def _row_spec(tm, width, colblk=0):
    return pl.BlockSpec((tm, width), lambda i: (i, colblk))


def _full_spec(shape):
    return pl.BlockSpec(shape, lambda i: (0,) * len(shape))


def _ssd_post(y, proj, x, p, ng, wo, wg, wp, gng, png):
    t = x.shape[0]
    tm = _tile(t, 512)
    return pl.pallas_call(
        _ssd_post_kernel,
        grid=(t // tm,),
        in_specs=[
            _row_spec(tm, SSD_INNER),
            _row_spec(tm, SSD_INNER, 0),
            _row_spec(tm, D_MODEL), _row_spec(tm, PLE_DIM),
            _full_spec((1, SSD_INNER)),
            _full_spec(wo.shape), _full_spec(wg.shape), _full_spec(wp.shape),
            _full_spec((1, D_MODEL)), _full_spec((1, D_MODEL)),
        ],
        out_specs=_row_spec(tm, D_MODEL),
        out_shape=jax.ShapeDtypeStruct((t, D_MODEL), F32),
        compiler_params=_cparams("parallel"),
        name="ssd_post",
    )(y, proj, x, p, ng, wo, wg, wp, gng, png)


def _att_post(o, proj, x, p, wo, wg, wp, gng, png):
    t = x.shape[0]
    tm = _tile(t, 512)
    return pl.pallas_call(
        _att_post_kernel,
        grid=(t // tm,),
        in_specs=[
            _row_spec(tm, D_MODEL),
            _row_spec(tm, D_MODEL, 3),
            _row_spec(tm, D_MODEL), _row_spec(tm, PLE_DIM),
            _full_spec(wo.shape), _full_spec(wg.shape), _full_spec(wp.shape),
            _full_spec((1, D_MODEL)), _full_spec((1, D_MODEL)),
        ],
        out_specs=_row_spec(tm, D_MODEL),
        out_shape=jax.ShapeDtypeStruct((t, D_MODEL), F32),
        compiler_params=_cparams("parallel"),
        name="att_post",
    )(o, proj, x, p, wo, wg, wp, gng, png)


def _attprep_kernel(q_ref, k_ref, v_ref, qg_ref, kg_ref, qn_ref, kt_ref, vb_ref):
    shape = q_ref.shape
    first = lax.broadcasted_iota(jnp.int32, shape, 1) < ATT_QK_DIM

    def norm(x, g):
        x2 = x * x
        s1 = jnp.sum(jnp.where(first, x2, 0.0), axis=-1, keepdims=True)
        s2 = jnp.sum(jnp.where(first, 0.0, x2), axis=-1, keepdims=True)
        ms = jnp.where(first, s1, s2) * (1.0 / ATT_QK_DIM)
        return x * lax.rsqrt(ms + EPS) * g

    qn_ref[0] = (norm(q_ref[...], qg_ref[...]) * (ATT_QK_DIM ** -0.5)).astype(BF16)
    kt_ref[0] = norm(k_ref[...], kg_ref[...]).T.astype(BF16)
    vb_ref[0] = v_ref[...].astype(BF16)


def _att_prep(proj, qg, kg, nb, seq):
    tl = _tile(seq, 512)
    nl = seq // tl
    nh = ATT_HEADS
    blk = lambda off: pl.BlockSpec((tl, LANES), lambda bb, h, i: (bb * nl + i, off + h))
    return pl.pallas_call(
        _attprep_kernel,
        grid=(nb, nh, nl),
        in_specs=[blk(0), blk(nh), blk(2 * nh), _g3((1, LANES)), _g3((1, LANES))],
        out_specs=[
            pl.BlockSpec((1, tl, LANES), lambda bb, h, i: (bb * nh + h, i, 0)),
            pl.BlockSpec((1, LANES, tl), lambda bb, h, i: (bb * nh + h, 0, i)),
            pl.BlockSpec((1, tl, LANES), lambda bb, h, i: (bb * nh + h, i, 0)),
        ],
        out_shape=[
            jax.ShapeDtypeStruct((nb * nh, seq, LANES), BF16),
            jax.ShapeDtypeStruct((nb * nh, LANES, seq), BF16),
            jax.ShapeDtypeStruct((nb * nh, seq, LANES), BF16),
        ],
        compiler_params=_cparams("parallel", "parallel", "parallel"),
        name="att_prep",
    )(proj, proj, proj, qg, kg)


def _g3(shape):
    return pl.BlockSpec(shape, lambda bb, h, i: (0,) * len(shape))


def _flash_kernel(q_ref, kt_ref, v_ref, slope_ref, lam_ref, sg_ref, o_ref, *, tk, nk, lambda_init):
    tq = q_ref.shape[1]
    q = q_ref[0]
    first = lax.broadcasted_iota(jnp.int32, q.shape, 1) < ATT_QK_DIM
    zero = jnp.zeros_like(q)
    q1 = jnp.where(first, q, zero)
    q2 = jnp.where(first, zero, q)
    slope = slope_ref[0][:, 0:1]
    rel = (lax.broadcasted_iota(jnp.int32, (tq, tk), 1)
           - lax.broadcasted_iota(jnp.int32, (tq, tk), 0))
    q0 = pl.program_id(2) * tq

    def update(qm, kt, vt, bias, m, l, acc):
        s = jnp.dot(qm, kt, preferred_element_type=F32) - bias
        m_new = jnp.maximum(m, jnp.max(s, axis=-1, keepdims=True))
        alpha = jnp.exp(m - m_new)
        p = jnp.exp(s - m_new)
        l = alpha * l + jnp.sum(p, axis=-1, keepdims=True)
        acc = alpha * acc + jnp.dot(p.astype(BF16), vt, preferred_element_type=F32)
        return m_new, l, acc

    def body(kidx, carry):
        m1, l1, a1, m2, l2, a2 = carry
        k0 = pl.multiple_of(kidx * tk, tk)
        kt = kt_ref[0, :, pl.ds(k0, tk)]
        vt = v_ref[0, pl.ds(k0, tk), :]
        bias = jnp.abs(rel + (k0 - q0)).astype(F32) * slope
        m1, l1, a1 = update(q1, kt, vt, bias, m1, l1, a1)
        m2, l2, a2 = update(q2, kt, vt, bias, m2, l2, a2)
        return m1, l1, a1, m2, l2, a2

    neg = jnp.full((tq, 1), -jnp.inf, F32)
    zl = jnp.zeros((tq, 1), F32)
    za = jnp.zeros((tq, ATT_V_DIM), F32)
    m1, l1, a1, m2, l2, a2 = lax.fori_loop(0, nk, body, (neg, zl, za, neg, zl, za))

    lv = lam_ref[...]
    lam = (jnp.exp(jnp.sum(lv[0:1] * lv[1:2], axis=-1, keepdims=True))
           - jnp.exp(jnp.sum(lv[2:3] * lv[3:4], axis=-1, keepdims=True)) + lambda_init)
    o = a1 / l1 - lam * (a2 / l2)
    o_ref[...] = _rms(o, sg_ref[...]) * (1.0 - lambda_init)


def _flash(qn, kt, vb, slopes, lamv, sg, nb, seq, lambda_init):
    nh = ATT_HEADS
    tq = _tile(seq, 512)
    tk = _tile(seq, 512)
    nq = seq // tq
    return pl.pallas_call(
        functools.partial(_flash_kernel, tk=tk, nk=seq // tk, lambda_init=lambda_init),
        grid=(nb, nh, nq),
        in_specs=[
            pl.BlockSpec((1, tq, LANES), lambda bb, h, i: (bb * nh + h, i, 0)),
            pl.BlockSpec((1, LANES, seq), lambda bb, h, i: (bb * nh + h, 0, 0)),
            pl.BlockSpec((1, seq, LANES), lambda bb, h, i: (bb * nh + h, 0, 0)),
            pl.BlockSpec((1, 1, LANES), lambda bb, h, i: (h, 0, 0)),
            _g3((4, ATT_QK_DIM)),
            _g3((1, ATT_V_DIM)),
        ],
        out_specs=pl.BlockSpec((tq, LANES), lambda bb, h, i: (bb * nq + i, h)),
        out_shape=jax.ShapeDtypeStruct((nb * seq, nh * ATT_V_DIM), F32),
        compiler_params=_cparams("parallel", "parallel", "arbitrary"),
        name="flash_diff_attn",
    )(qn, kt, vb, slopes, lamv, sg)


FLASH_TILE = 512
POS_U = 0
POS_W_LOW = 2
POS_OFF = 4
POS_SHIFT = 7
POS_W_UP = 8
BAND_EXP_CUTOFF = 112.0
SHIFT_SAFE_MAX = 40.0


def _split2(x):
    hi = x.astype(BF16).astype(F32)
    lo = (x - hi).astype(BF16).astype(F32)
    return hi, lo


def _split3(x):
    hi = x.astype(BF16).astype(F32)
    r = x - hi
    mid = r.astype(BF16).astype(F32)
    lo = (r - mid).astype(BF16).astype(F32)
    return hi, mid, lo


def _attprep_band_kernel(q_ref, k_ref, v_ref, qg_ref, kg_ref, slope_ref, qn_ref, kt_ref, vb_ref):
    tl = q_ref.shape[0]
    gr = lax.broadcasted_iota(jnp.int32, (3 * LANES, LANES), 0) % LANES
    gc = lax.broadcasted_iota(jnp.int32, (3 * LANES, LANES), 1)
    same_half = jnp.where((gr < ATT_QK_DIM) == (gc < ATT_QK_DIM), 1.0, 0.0).astype(BF16)

    def norm(x, g):
        pieces = jnp.concatenate(_split3(x * x), axis=1).astype(BF16)
        ms = jnp.dot(pieces, same_half, preferred_element_type=F32) * (1.0 / ATT_QK_DIM)
        return x * lax.rsqrt(ms + EPS) * g

    mq = jnp.max(jnp.abs(qg_ref[...]), axis=-1, keepdims=True)
    mk = jnp.max(jnp.abs(kg_ref[...]), axis=-1, keepdims=True)
    shift = mq * mk * (ATT_QK_DIM ** 0.5 * 1.02)

    r = lax.broadcasted_iota(jnp.int32, (LANES, tl), 0)
    j = lax.broadcasted_iota(jnp.int32, (LANES, tl), 1).astype(F32)
    minus_one = (r == POS_U) | (r == POS_U + 1) | ((r >= POS_OFF) & (r < POS_OFF + 3))
    pos_const = jnp.where(r == POS_SHIFT, -shift, jnp.where(minus_one, -1.0, 0.0))
    ones = jnp.ones((tl, LANES), BF16)

    for h in range(ATT_HEADS):
        hs = slice(h * LANES, (h + 1) * LANES)
        qn_ref[h] = (norm(q_ref[:, hs], qg_ref[...]) * (ATT_QK_DIM ** -0.5)).astype(BF16)
        kt_ref[h, 0:LANES, :] = norm(k_ref[:, hs], kg_ref[...]).T.astype(BF16)
        slope = slope_ref[h][:, 0:1]
        wl_hi, wl_lo = _split2(slope * (tl - 1.0 - j))
        wu_hi, wu_lo = _split2(slope * j)
        pos = jnp.where(r == POS_W_LOW, -wl_hi, pos_const)
        pos = jnp.where(r == POS_W_LOW + 1, -wl_lo, pos)
        pos = jnp.where(r == POS_W_UP, -wu_hi, pos)
        pos = jnp.where(r == POS_W_UP + 1, -wu_lo, pos)
        kt_ref[h, LANES:2 * LANES, :] = pos.astype(BF16)
        vb_ref[h, :, 0:LANES] = v_ref[:, hs].astype(BF16)
        vb_ref[h, :, LANES:2 * LANES] = ones


def _att_prep_band(proj, qg, kg, slopes, nb, seq):
    tl = _tile(seq, FLASH_TILE)
    nl = seq // tl
    nh = ATT_HEADS
    w = nh * LANES
    blk = lambda c: pl.BlockSpec((tl, w), lambda bb, i: (bb * nl + i, c))
    full = lambda shape: pl.BlockSpec(shape, lambda bb, i: (0,) * len(shape))
    return pl.pallas_call(
        _attprep_band_kernel,
        grid=(nb, nl),
        in_specs=[blk(0), blk(1), blk(2), full((1, LANES)), full((1, LANES)),
                  full((nh, 1, LANES))],
        out_specs=[
            pl.BlockSpec((nh, tl, LANES), lambda bb, i: (bb, i, 0)),
            pl.BlockSpec((nh, 2 * LANES, tl), lambda bb, i: (bb, 0, i)),
            pl.BlockSpec((nh, tl, 2 * LANES), lambda bb, i: (bb, i, 0)),
        ],
        out_shape=[
            jax.ShapeDtypeStruct((nb * nh, seq, LANES), BF16),
            jax.ShapeDtypeStruct((nb * nh, 2 * LANES, seq), BF16),
            jax.ShapeDtypeStruct((nb * nh, seq, 2 * LANES), BF16),
        ],
        compiler_params=_cparams("parallel", "parallel"),
        name="att_prep_band",
    )(proj, proj, proj, qg, kg, slopes)


def _flash_band_kernel(q_ref, kt_ref, v_ref, slope_ref, lam_ref, sg_ref, o_ref,
                       lhs_ref, acc_ref, p_ref, base_ref, bias_ref, *, tk, nk, lambda_init):
    tq = q_ref.shape[1]
    assert tq == tk
    h = pl.program_id(1)
    qi = pl.program_id(2)
    q0 = qi * tq
    slope_row = slope_ref[0]
    lane = lax.broadcasted_iota(jnp.int32, (tq, LANES), 1)
    rowf = lax.broadcasted_iota(jnp.int32, (tq, LANES), 0).astype(F32)

    q = q_ref[0]
    zero = jnp.zeros_like(q)
    first = lane < ATT_QK_DIM
    lhs_ref[0:tq] = jnp.where(first, q, zero)
    lhs_ref[tq:2 * tq] = jnp.where(first, zero, q)
    acc_ref[...] = jnp.zeros(acc_ref.shape, F32)

    def pos_base(u, w_lane):
        u_hi, u_lo = _split2(u)
        ones = (lane == POS_SHIFT) | (lane == w_lane) | (lane == w_lane + 1)
        base = jnp.where(ones, 1.0, 0.0)
        base = jnp.where(lane == POS_U, u_hi, base)
        return jnp.where(lane == POS_U + 1, u_lo, base)

    off_lanes = (lane >= POS_OFF) & (lane < POS_OFF + 3)
    lane1 = lax.broadcasted_iota(jnp.int32, (2 * SUBLANES, LANES), 1)
    slope_tile = jnp.broadcast_to(slope_row, (2 * SUBLANES, LANES))

    def pos_half(base, dist):
        if dist is not None:
            hi, mid, lo = _split3(slope_tile * dist.astype(F32))
            orow = jnp.where(lane1 == POS_OFF, hi, jnp.where(lane1 == POS_OFF + 1, mid, lo))
            base = jnp.where(off_lanes, orow[0:1, :], base)
        return base.astype(BF16)

    halves = (slice(0, tq), slice(tq, 2 * tq))

    def raw_scores(kidx, pos):
        k0 = pl.multiple_of(kidx * tk, tk)
        lhs = jnp.concatenate([jnp.concatenate([lhs_ref[rows], pos], axis=1) for rows in halves],
                              axis=0)
        return jnp.dot(lhs, kt_ref[0, :, pl.ds(k0, tk)], preferred_element_type=F32)

    def weighted_sum(kidx, slot):
        k0 = pl.multiple_of(kidx * tk, tk)
        acc_ref[...] += jnp.dot(p_ref[slot], v_ref[0, pl.ds(k0, tk), :], preferred_element_type=F32)


    band = lax.shift_left(jnp.int32(int(BAND_EXP_CUTOFF)), h + 1)
    k_lo = jnp.maximum(q0 + 1 - band, 0) // tk
    k_hi = jnp.minimum((q0 + tq - 2 + band + tk) // tk, nk)
    n_low = qi - k_lo
    n_off_diag = n_low + (k_hi - qi - 1)

    @pl.when(qi == 0)
    def _():
        base_ref[0] = pos_base(slope_row * rowf, POS_W_LOW)
        base_ref[1] = pos_base(slope_row * (tq - 1.0 - rowf), POS_W_UP)
        rel = (lax.broadcasted_iota(jnp.int32, (tq, tk), 1)
               - lax.broadcasted_iota(jnp.int32, (tq, tk), 0))
        bias_ref[...] = jnp.abs(rel).astype(F32) * slope_row[:, 0:1]

    s = raw_scores(qi, pos_half(jnp.where(lane == POS_SHIFT, 1.0, 0.0), None))
    for rows in halves:
        p_ref[0, rows] = jnp.exp(s[rows] - bias_ref[...]).astype(BF16)

    def tile_of(u):
        return jnp.where(u < n_low, k_lo + u, qi + 1 + u - n_low)

    def step(u, prev, s_in, s_out):
        below = u < n_low
        kidx = tile_of(u)
        dist = jnp.where(below, q0 - kidx * tk - (tk - 1), kidx * tk - q0 - (tq - 1))
        weighted_sum(prev, s_in)
        s = raw_scores(kidx, pos_half(base_ref[jnp.where(below, 0, 1)], dist))
        p_ref[s_out] = jnp.exp(s).astype(BF16)
        return kidx

    def run(u0, n_steps, prev):
        for j in range(n_steps):
            prev = step(u0 + j, prev, j, (j + 1) % n_steps)
        return prev

    done = 0
    prev = qi
    for width in (8, 4, 2):
        n_bodies = (n_off_diag - done) // width
        prev = lax.fori_loop(0, n_bodies,
                             lambda i, p, w=width, d=done: run(d + w * i, w, p), prev)
        done = done + width * n_bodies
    odd = n_off_diag - done

    @pl.when(odd == 1)
    def _():
        weighted_sum(step(n_off_diag - 1, prev, 0, 1), 1)

    @pl.when(odd == 0)
    def _():
        weighted_sum(prev, 0)

    lv = lam_ref[...]
    lam = (jnp.exp(jnp.sum(lv[0:1] * lv[1:2], axis=-1, keepdims=True))
           - jnp.exp(jnp.sum(lv[2:3] * lv[3:4], axis=-1, keepdims=True)) + lambda_init)
    a1 = acc_ref[0:tq, 0:LANES]
    l1 = acc_ref[0:tq, LANES:2 * LANES]
    a2 = acc_ref[tq:2 * tq, 0:LANES]
    l2 = acc_ref[tq:2 * tq, LANES:2 * LANES]
    o = a1 / l1 - lam * (a2 / l2)
    o_ref[...] = _rms(o, sg_ref[...]) * (1.0 - lambda_init)


def _flash_band(qn, kt, vb, slopes, lamv, sg, nb, seq, lambda_init):
    nh = ATT_HEADS
    tq = _tile(seq, FLASH_TILE)
    nq = seq // tq
    return pl.pallas_call(
        functools.partial(_flash_band_kernel, tk=tq, nk=nq, lambda_init=lambda_init),
        grid=(nb, nh, nq),
        in_specs=[
            pl.BlockSpec((1, tq, LANES), lambda bb, h, i: (bb * nh + h, i, 0)),
            pl.BlockSpec((1, 2 * LANES, seq), lambda bb, h, i: (bb * nh + h, 0, 0)),
            pl.BlockSpec((1, seq, 2 * LANES), lambda bb, h, i: (bb * nh + h, 0, 0)),
            pl.BlockSpec((1, 1, LANES), lambda bb, h, i: (h, 0, 0)),
            _g3((4, ATT_QK_DIM)),
            _g3((1, ATT_V_DIM)),
        ],
        out_specs=pl.BlockSpec((tq, LANES), lambda bb, h, i: (bb * nq + i, h)),
        out_shape=jax.ShapeDtypeStruct((nb * seq, nh * ATT_V_DIM), F32),
        scratch_shapes=[
            pltpu.VMEM((2 * tq, LANES), BF16),
            pltpu.VMEM((2 * tq, 2 * LANES), F32),
            pltpu.VMEM((8, 2 * tq, tq), BF16),
            pltpu.VMEM((2, tq, LANES), F32),
            pltpu.VMEM((tq, tq), F32),
        ],
        compiler_params=_cparams("parallel", "parallel", "arbitrary"),
        name="flash_band",
    )(qn, kt, vb, slopes, lamv, sg)


def _attention(proj, lw, slopes, nb, seq, lambda_init):
    def banded(proj):
        qn, kt, vb = _att_prep_band(proj, lw["qg"], lw["kg"], slopes, nb, seq)
        return _flash_band(qn, kt, vb, slopes, lw["lam"], lw["sub_g"], nb, seq, lambda_init)

    def online(proj):
        qn, kt, vb = _att_prep(proj, lw["qg"], lw["kg"], nb, seq)
        return _flash(qn, kt, vb, slopes, lw["lam"], lw["sub_g"], nb, seq, lambda_init)

    shift = jnp.max(jnp.abs(lw["qg"])) * jnp.max(jnp.abs(lw["kg"])) * (ATT_QK_DIM ** 0.5 * 1.02)
    return lax.cond(shift < SHIFT_SAFE_MAX, banded, online, proj)


def _alibi_slopes():
    s = np.array([2.0 ** (-8.0 * (i + 1) / ATT_HEADS) for i in range(ATT_HEADS)], dtype=np.float32)
    return jnp.asarray(np.broadcast_to(s[:, None, None], (ATT_HEADS, 1, LANES)).copy())


def _prep_weights(w):
    row = lambda v: v.reshape(1, -1).astype(F32)
    pad_lanes = lambda v: jnp.pad(v.reshape(1, -1).astype(F32), ((0, 0), (0, LANES - v.size)))
    out = {"layers": [], "slopes": _alibi_slopes()}
    for i in range(DEPTH):
        j = i // 2
        lw = {
            "pre_g": row(w["pre_norm_g"][i]),
            "wg": w["ple_w_gate"][i].astype(BF16),
            "wp": w["ple_w_proj"][i].astype(BF16),
            "gng": row(w["ple_gate_norm_g"][i]),
            "png": row(w["ple_norm_g"][i]),
        }
        if i % 2 == 0:
            win = w["ssd_w_in"][j]
            lw.update(
                w_main=win[:, :SSD_MAIN_COLS].astype(BF16),
                w_dt=jnp.pad(win[:, SSD_MAIN_COLS:], ((0, 0), (0, LANES - 2 * SSD_HEADS))).astype(BF16),
                dt_bias=pad_lanes(w["ssd_dt_bias"][j]),
                a_log=pad_lanes(w["ssd_a_log"][j]),
                conv_w=w["ssd_conv_w"][j].astype(F32),
                conv_b=row(w["ssd_conv_b"][j]),
                d_skip=row(jnp.repeat(w["ssd_d_skip"][j], SSD_HEAD_DIM)),
                norm_g=row(w["ssd_norm_g"][j]),
                w_out=w["ssd_w_out"][j].astype(BF16),
            )
        else:
            lw.update(
                w_in=w["att_w_in"][j].astype(BF16),
                qg=row(jnp.tile(w["att_q_norm_g"][j], 2)),
                kg=row(jnp.tile(w["att_k_norm_g"][j], 2)),
                lam=jnp.stack([w["att_lam_q1"][j], w["att_lam_k1"][j],
                               w["att_lam_q2"][j], w["att_lam_k2"][j]]).astype(F32),
                sub_g=row(w["att_sub_norm_g"][j]),
                w_out=w["att_w_out"][j].astype(BF16),
            )
        out["layers"].append(lw)
    return out


def _trunk(x3, p4, pw):
    nb, seq, d = x3.shape
    t = nb * seq
    x = x3.reshape(t, d)
    for i in range(DEPTH):
        lw = pw["layers"][i]
        p = p4[i].reshape(t, PLE_DIM)
        if i % 2 == 0:
            proj, dt = _inproj(x, lw["pre_g"], lw["w_main"], lw["w_dt"], lw["dt_bias"])
            xs = _conv_silu(proj, lw["conv_w"], lw["conv_b"], nb, seq,
                            SSD_INNER, 0, SSD_INNER, F32)
            bc = _conv_silu(proj, lw["conv_w"], lw["conv_b"], nb, seq,
                            2 * SSD_INNER, SSD_INNER, 2 * SSD_GN, BF16)
            yf = _ssd_scan(xs, bc, dt, lw["a_log"], nb, seq)
            y = _ssd_scan(xs, bc, dt, lw["a_log"], nb, seq, yf, lw["d_skip"])
            x = _ssd_post(y, proj, x, p, lw["norm_g"], lw["w_out"],
                          lw["wg"], lw["wp"], lw["gng"], lw["png"])
        else:
            lambda_init = 0.8 - 0.6 * math.exp(-0.3 * i)
            proj = _inproj(x, lw["pre_g"], lw["w_in"])
            o = _attention(proj, lw, pw["slopes"], nb, seq, lambda_init)
            x = _att_post(o, proj, x, p, lw["w_out"], lw["wg"], lw["wp"], lw["gng"], lw["png"])
    return x.reshape(nb, seq, d)


def kernel(x_prompt, x_sample, p_prompt, p_sample, pre_norm_g, ssd_w_in, ssd_conv_w, ssd_conv_b, ssd_dt_bias, ssd_a_log, ssd_d_skip, ssd_norm_g, ssd_w_out, att_w_in, att_q_norm_g, att_k_norm_g, att_lam_q1, att_lam_k1, att_lam_q2, att_lam_k2, att_sub_norm_g, att_w_out, ple_w_proj, ple_norm_g, ple_gate_norm_g, ple_w_gate):
    pw = _prep_weights(dict(
        pre_norm_g=pre_norm_g, ssd_w_in=ssd_w_in, ssd_conv_w=ssd_conv_w, ssd_conv_b=ssd_conv_b,
        ssd_dt_bias=ssd_dt_bias, ssd_a_log=ssd_a_log, ssd_d_skip=ssd_d_skip, ssd_norm_g=ssd_norm_g,
        ssd_w_out=ssd_w_out, att_w_in=att_w_in, att_q_norm_g=att_q_norm_g, att_k_norm_g=att_k_norm_g,
        att_lam_q1=att_lam_q1, att_lam_k1=att_lam_k1, att_lam_q2=att_lam_q2, att_lam_k2=att_lam_k2,
        att_sub_norm_g=att_sub_norm_g, att_w_out=att_w_out, ple_w_proj=ple_w_proj,
        ple_norm_g=ple_norm_g, ple_gate_norm_g=ple_gate_norm_g, ple_w_gate=ple_w_gate))
    return (_trunk(x_prompt, p_prompt, pw), _trunk(x_sample, p_sample, pw))
```

```python
import functools
import math

import numpy as np
import jax
import jax.numpy as jnp
from jax import lax
from jax.experimental import pallas as pl
from jax.experimental.pallas import tpu as pltpu

F32 = jnp.float32
BF16 = jnp.bfloat16

EPS = 1e-6
D_MODEL = 1024
DEPTH = 4
PLE_DIM = 256

SSD_INNER = 2048
SSD_HEAD_DIM = 64
SSD_HEADS = 32
SSD_GROUPS = 8
SSD_HPG = 4
SSD_STATE = 128
SSD_GN = SSD_GROUPS * SSD_STATE
SSD_CONV_W = 5
SSD_CHUNK = 128
SSD_CHUNKS_PER_STEP = 4
SSD_MAIN_COLS = SSD_INNER + SSD_INNER + 2 * SSD_GN

ATT_HEADS = 8
ATT_QK_DIM = 64
ATT_V_DIM = 128
ATT_IN_COLS = 4096

LANES = 128
SUBLANES = 8
VMEM_LIMIT_BYTES = 56 * 1024 * 1024


def _cparams(*sem):
    return pltpu.CompilerParams(dimension_semantics=sem, vmem_limit_bytes=VMEM_LIMIT_BYTES)


def _tile(n, pref):
    t = min(n, pref)
    assert n % t == 0, (n, pref)
    return t


def _rms(x, g):
    return x * lax.rsqrt(jnp.mean(x * x, axis=-1, keepdims=True) + EPS) * g


def _silu(x):
    hx = 0.5 * x
    return hx + hx * jnp.tanh(hx)


def _inproj_kernel(x_ref, g_ref, w_ref, *rest, has_dt):
    if has_dt:
        wdt_ref, dtb_ref, out_ref, dt_ref, hn_ref = rest
    else:
        out_ref, hn_ref = rest

    @pl.when(pl.program_id(1) == 0)
    def _():
        hn_ref[...] = _rms(x_ref[...], g_ref[...]).astype(BF16)
        if has_dt:
            raw = jnp.dot(hn_ref[...], wdt_ref[...], preferred_element_type=F32) + dtb_ref[...]
            dt_ref[...] = jax.nn.softplus(raw)

    out_ref[...] = jnp.dot(hn_ref[...], w_ref[...], preferred_element_type=F32)


def _inproj(x, g, w, wdt=None, dtb=None):
    t, d = x.shape
    n = w.shape[1]
    tm = _tile(t, 1024)
    tn = _tile(n, 2048)
    has_dt = wdt is not None
    in_specs = [
        pl.BlockSpec((tm, d), lambda i, j: (i, 0)),
        pl.BlockSpec((1, d), lambda i, j: (0, 0)),
        pl.BlockSpec((d, tn), lambda i, j: (0, j)),
    ]
    args = [x, g, w]
    out_shape = [jax.ShapeDtypeStruct((t, n), F32)]
    out_specs = [pl.BlockSpec((tm, tn), lambda i, j: (i, j))]
    if has_dt:
        in_specs += [pl.BlockSpec((d, LANES), lambda i, j: (0, 0)),
                     pl.BlockSpec((1, LANES), lambda i, j: (0, 0))]
        args += [wdt, dtb]
        out_shape.append(jax.ShapeDtypeStruct((t, LANES), F32))
        out_specs.append(pl.BlockSpec((tm, LANES), lambda i, j: (i, 0)))
    res = pl.pallas_call(
        functools.partial(_inproj_kernel, has_dt=has_dt),
        grid=(t // tm, n // tn),
        in_specs=in_specs,
        out_specs=out_specs,
        out_shape=out_shape,
        scratch_shapes=[pltpu.VMEM((tm, d), BF16)],
        compiler_params=_cparams("parallel", "arbitrary"),
        name="inproj_dt" if has_dt else "inproj",
    )(*args)
    return res if has_dt else res[0]


def _conv_kernel(prev_ref, cur_ref, next_ref, w_ref, b_ref, out_ref, *, nl):
    i = pl.program_id(1)
    tl = cur_ref.shape[0]
    prev = jnp.where(i > 0, prev_ref[...], 0.0)
    nxt = jnp.where(i < nl - 1, next_ref[...], 0.0)
    ext = jnp.concatenate([prev, cur_ref[...], nxt], axis=0)
    n = tl + 2 * SUBLANES
    acc = jnp.zeros(cur_ref.shape, F32) + b_ref[...]
    half = SSD_CONV_W // 2
    for k in range(SSD_CONV_W):
        shift = (half - k) % n
        r = ext if shift == 0 else pltpu.roll(ext, shift, 0)
        acc = acc + r[SUBLANES:SUBLANES + tl] * w_ref[k:k + 1, :]
    out_ref[...] = _silu(acc).astype(out_ref.dtype)


def _conv_silu(proj, w, b, nb, seq, col0_in, col0_w, ncols, out_dtype):
    t = proj.shape[0]
    cb = 512
    tl = _tile(seq, 512)
    nl = seq // tl
    rb = tl // SUBLANES
    nrb = t // SUBLANES
    ci, cw = col0_in // cb, col0_w // cb
    return pl.pallas_call(
        functools.partial(_conv_kernel, nl=nl),
        grid=(nb, nl, ncols // cb),
        in_specs=[
            pl.BlockSpec((SUBLANES, cb), lambda bb, i, c: (jnp.maximum((bb * nl + i) * rb - 1, 0), ci + c)),
            pl.BlockSpec((tl, cb), lambda bb, i, c: (bb * nl + i, ci + c)),
            pl.BlockSpec((SUBLANES, cb), lambda bb, i, c: (jnp.minimum((bb * nl + i + 1) * rb, nrb - 1), ci + c)),
            pl.BlockSpec((SSD_CONV_W, cb), lambda bb, i, c: (0, cw + c)),
            pl.BlockSpec((1, cb), lambda bb, i, c: (0, cw + c)),
        ],
        out_specs=pl.BlockSpec((tl, cb), lambda bb, i, c: (bb * nl + i, c)),
        out_shape=jax.ShapeDtypeStruct((t, ncols), out_dtype),
        compiler_params=_cparams("parallel", "parallel", "parallel"),
        name="conv_silu",
    )(proj, proj, proj, w, b)


def _ssd_spread_table(reverse):
    lane0 = SSD_HEADS if reverse else 0
    ee = np.zeros((2 * LANES, SSD_INNER), np.float32)
    for h in range(SSD_HEADS):
        for piece in range(2):
            ee[piece * LANES + lane0 + h, h * SSD_HEAD_DIM:(h + 1) * SSD_HEAD_DIM] = 1.0
    return jnp.asarray(ee, BF16)


def _ssd_kernel(xs_ref, b_ref, c_ref, dt_ref, alog_ref, ee_ref, *rest, reverse):
    q = SSD_CHUNK
    gw = SSD_HPG * SSD_HEAD_DIM
    if reverse:
        yf_ref, dsk_ref, y_ref, state_ref = rest
    else:
        y_ref, state_ref = rest

    @pl.when(pl.program_id(1) == 0)
    def _():
        state_ref[...] = jnp.zeros(state_ref.shape, F32)

    lane0 = SSD_HEADS if reverse else 0
    row = lax.broadcasted_iota(jnp.int32, (q, q), 0)
    col = lax.broadcasted_iota(jnp.int32, (q, q), 1)
    mask = (row <= col) if reverse else (row >= col)
    tri = jnp.where(mask, 1.0, 0.0).astype(F32)
    head_of_lane = lax.broadcasted_iota(jnp.int32, (q, gw), 1) // SSD_HEAD_DIM

    pieces2 = lambda x: jnp.concatenate(_split2(x), axis=1).astype(BF16)

    def chunk(rows):
        dtc = dt_ref[rows]
        a = dtc * (-jnp.exp(alog_ref[...]))
        c = jnp.dot(tri, a, precision=lax.Precision.HIGHEST, preferred_element_type=F32)
        c_t = c.T
        tot = c[0:1, :] if reverse else c[q - 1:q, :]
        ec = jnp.exp(c)
        ed = jnp.exp(tot - c)

        lhs_w = jnp.concatenate([pieces2(dtc), pieces2(dtc * ed), pieces2(ec)], axis=0)
        spread = jnp.dot(lhs_w, ee_ref[...], preferred_element_type=F32)

        for g in range(SSD_GROUPS):
            gs = slice(g * gw, (g + 1) * gw)
            bg = b_ref[rows, g * SSD_STATE:(g + 1) * SSD_STATE]
            cg = c_ref[rows, g * SSD_STATE:(g + 1) * SSD_STATE]
            dt_e, dted_e, ec_e = spread[0:q, gs], spread[q:2 * q, gs], spread[2 * q:3 * q, gs]
            xs_g = xs_ref[rows, gs]
            xdt = (xs_g * dt_e).astype(BF16)
            xw = (xs_g * dted_e).astype(BF16)
            cb = lax.dot_general(cg, bg, (((1,), (1,)), ((), ())), preferred_element_type=F32)
            bg_t = bg.astype(F32).T.astype(BF16)
            st_new = jnp.dot(bg_t, xw, preferred_element_type=F32)
            prev = state_ref[g]
            y_off = jnp.dot(cg, prev.astype(BF16), preferred_element_type=F32) * ec_e
            etot_e = ec_e[0:1, :] if reverse else ec_e[q - 1:q, :]
            state_ref[g] = prev * etot_e + st_new
            ms, xd = [], []
            for r in range(SSD_HPG):
                h = g * SSD_HPG + r
                seg = c[:, lane0 + h:lane0 + h + 1] - c_t[lane0 + h:lane0 + h + 1, :]
                lmat = jnp.exp(jnp.where(mask, seg, -jnp.inf))
                ms.append((cb * lmat).astype(BF16))
                xd.append(jnp.where(head_of_lane == r, xdt, jnp.zeros_like(xdt)))
            y_diag = jnp.dot(jnp.concatenate(ms, axis=1), jnp.concatenate(xd, axis=0),
                             preferred_element_type=F32)
            y_g = y_diag + y_off
            if reverse:
                y_g = yf_ref[rows, gs] + y_g + xs_g * dsk_ref[:, gs]
            y_ref[rows, gs] = y_g

    n_chunks = dt_ref.shape[0] // q
    for ci in (reversed(range(n_chunks)) if reverse else range(n_chunks)):
        chunk(slice(ci * q, (ci + 1) * q))


def _ssd_scan(xs, bc, dt, alog, nb, seq, yf=None, dsk=None):
    reverse = yf is not None
    t = xs.shape[0]
    q = _tile(seq, SSD_CHUNKS_PER_STEP * SSD_CHUNK)
    nc = seq // q
    ee = _ssd_spread_table(reverse)

    def rblk(bb, j):
        return bb * nc + ((nc - 1 - j) if reverse else j)

    row_spec = pl.BlockSpec((q, SSD_INNER), lambda bb, j: (rblk(bb, j), 0))
    in_specs = [
        row_spec,
        pl.BlockSpec((q, SSD_GN), lambda bb, j: (rblk(bb, j), 0)),
        pl.BlockSpec((q, SSD_GN), lambda bb, j: (rblk(bb, j), 1)),
        pl.BlockSpec((q, LANES), lambda bb, j: (rblk(bb, j), 0)),
        pl.BlockSpec((1, LANES), lambda bb, j: (0, 0)),
        pl.BlockSpec(ee.shape, lambda bb, j: (0, 0)),
    ]
    args = [xs, bc, bc, dt, alog, ee]
    if reverse:
        in_specs += [row_spec, pl.BlockSpec((1, SSD_INNER), lambda bb, j: (0, 0))]
        args += [yf, dsk]
    return pl.pallas_call(
        functools.partial(_ssd_kernel, reverse=reverse),
        grid=(nb, nc),
        in_specs=in_specs,
        out_specs=row_spec,
        out_shape=jax.ShapeDtypeStruct((t, SSD_INNER), F32),
        scratch_shapes=[pltpu.VMEM((SSD_GROUPS, SSD_STATE, SSD_HPG * SSD_HEAD_DIM), F32)],
        compiler_params=_cparams("parallel", "arbitrary"),
        name="ssd_scan_bwd" if reverse else "ssd_scan_fwd",
    )(*args)


def _ple_tail(x, mix, p_ref, wg_ref, wp_ref, gng_ref, png_ref, out_ref):
    x1 = x + mix
    gate = jax.nn.sigmoid(
        jnp.dot(_rms(x1, gng_ref[...]).astype(BF16), wg_ref[...], preferred_element_type=F32))
    e = _rms(jnp.dot(p_ref[...].astype(BF16), wp_ref[...], preferred_element_type=F32), png_ref[...])
    out_ref[...] = x1 + gate * e


def _ssd_post_kernel(y_ref, z_ref, x_ref, p_ref, ng_ref, wo_ref,
                     wg_ref, wp_ref, gng_ref, png_ref, out_ref):
    y = _rms(y_ref[...] * _silu(z_ref[...]), ng_ref[...])
    mix = jnp.dot(y.astype(BF16), wo_ref[...], preferred_element_type=F32)
    _ple_tail(x_ref[...], mix, p_ref, wg_ref, wp_ref, gng_ref, png_ref, out_ref)


def _att_post_kernel(o_ref, gate_ref, x_ref, p_ref, wo_ref, wg_ref, wp_ref, gng_ref, png_ref, out_ref):
    u = o_ref[...] * _silu(gate_ref[...])
    mix = jnp.dot(u.astype(BF16), wo_ref[...], preferred_element_type=F32)
    _ple_tail(x_ref[...], mix, p_ref, wg_ref, wp_ref, gng_ref, png_ref, out_ref)


def _row_spec(tm, width, colblk=0):
    return pl.BlockSpec((tm, width), lambda i: (i, colblk))


def _full_spec(shape):
    return pl.BlockSpec(shape, lambda i: (0,) * len(shape))


def _ssd_post(y, proj, x, p, ng, wo, wg, wp, gng, png):
    t = x.shape[0]
    tm = _tile(t, 512)
    return pl.pallas_call(
        _ssd_post_kernel,
        grid=(t // tm,),
        in_specs=[
            _row_spec(tm, SSD_INNER),
            _row_spec(tm, SSD_INNER, 0),
            _row_spec(tm, D_MODEL), _row_spec(tm, PLE_DIM),
            _full_spec((1, SSD_INNER)),
            _full_spec(wo.shape), _full_spec(wg.shape), _full_spec(wp.shape),
            _full_spec((1, D_MODEL)), _full_spec((1, D_MODEL)),
        ],
        out_specs=_row_spec(tm, D_MODEL),
        out_shape=jax.ShapeDtypeStruct((t, D_MODEL), F32),
        compiler_params=_cparams("parallel"),
        name="ssd_post",
    )(y, proj, x, p, ng, wo, wg, wp, gng, png)


def _att_post(o, proj, x, p, wo, wg, wp, gng, png):
    t = x.shape[0]
    tm = _tile(t, 512)
    return pl.pallas_call(
        _att_post_kernel,
        grid=(t // tm,),
        in_specs=[
            _row_spec(tm, D_MODEL),
            _row_spec(tm, D_MODEL, 3),
            _row_spec(tm, D_MODEL), _row_spec(tm, PLE_DIM),
            _full_spec(wo.shape), _full_spec(wg.shape), _full_spec(wp.shape),
            _full_spec((1, D_MODEL)), _full_spec((1, D_MODEL)),
        ],
        out_specs=_row_spec(tm, D_MODEL),
        out_shape=jax.ShapeDtypeStruct((t, D_MODEL), F32),
        compiler_params=_cparams("parallel"),
        name="att_post",
    )(o, proj, x, p, wo, wg, wp, gng, png)


def _attprep_kernel(q_ref, k_ref, v_ref, qg_ref, kg_ref, qn_ref, kt_ref, vb_ref):
    shape = q_ref.shape
    first = lax.broadcasted_iota(jnp.int32, shape, 1) < ATT_QK_DIM

    def norm(x, g):
        x2 = x * x
        s1 = jnp.sum(jnp.where(first, x2, 0.0), axis=-1, keepdims=True)
        s2 = jnp.sum(jnp.where(first, 0.0, x2), axis=-1, keepdims=True)
        ms = jnp.where(first, s1, s2) * (1.0 / ATT_QK_DIM)
        return x * lax.rsqrt(ms + EPS) * g

    qn_ref[0] = (norm(q_ref[...], qg_ref[...]) * (ATT_QK_DIM ** -0.5)).astype(BF16)
    kt_ref[0] = norm(k_ref[...], kg_ref[...]).T.astype(BF16)
    vb_ref[0] = v_ref[...].astype(BF16)


def _att_prep(proj, qg, kg, nb, seq):
    tl = _tile(seq, 512)
    nl = seq // tl
    nh = ATT_HEADS
    blk = lambda off: pl.BlockSpec((tl, LANES), lambda bb, h, i: (bb * nl + i, off + h))
    return pl.pallas_call(
        _attprep_kernel,
        grid=(nb, nh, nl),
        in_specs=[blk(0), blk(nh), blk(2 * nh), _g3((1, LANES)), _g3((1, LANES))],
        out_specs=[
            pl.BlockSpec((1, tl, LANES), lambda bb, h, i: (bb * nh + h, i, 0)),
            pl.BlockSpec((1, LANES, tl), lambda bb, h, i: (bb * nh + h, 0, i)),
            pl.BlockSpec((1, tl, LANES), lambda bb, h, i: (bb * nh + h, i, 0)),
        ],
        out_shape=[
            jax.ShapeDtypeStruct((nb * nh, seq, LANES), BF16),
            jax.ShapeDtypeStruct((nb * nh, LANES, seq), BF16),
            jax.ShapeDtypeStruct((nb * nh, seq, LANES), BF16),
        ],
        compiler_params=_cparams("parallel", "parallel", "parallel"),
        name="att_prep",
    )(proj, proj, proj, qg, kg)


def _g3(shape):
    return pl.BlockSpec(shape, lambda bb, h, i: (0,) * len(shape))


def _flash_kernel(q_ref, kt_ref, v_ref, slope_ref, lam_ref, sg_ref, o_ref, *, tk, nk, lambda_init):
    tq = q_ref.shape[1]
    q = q_ref[0]
    first = lax.broadcasted_iota(jnp.int32, q.shape, 1) < ATT_QK_DIM
    zero = jnp.zeros_like(q)
    q1 = jnp.where(first, q, zero)
    q2 = jnp.where(first, zero, q)
    slope = slope_ref[0][:, 0:1]
    rel = (lax.broadcasted_iota(jnp.int32, (tq, tk), 1)
           - lax.broadcasted_iota(jnp.int32, (tq, tk), 0))
    q0 = pl.program_id(2) * tq

    def update(qm, kt, vt, bias, m, l, acc):
        s = jnp.dot(qm, kt, preferred_element_type=F32) - bias
        m_new = jnp.maximum(m, jnp.max(s, axis=-1, keepdims=True))
        alpha = jnp.exp(m - m_new)
        p = jnp.exp(s - m_new)
        l = alpha * l + jnp.sum(p, axis=-1, keepdims=True)
        acc = alpha * acc + jnp.dot(p.astype(BF16), vt, preferred_element_type=F32)
        return m_new, l, acc

    def body(kidx, carry):
        m1, l1, a1, m2, l2, a2 = carry
        k0 = pl.multiple_of(kidx * tk, tk)
        kt = kt_ref[0, :, pl.ds(k0, tk)]
        vt = v_ref[0, pl.ds(k0, tk), :]
        bias = jnp.abs(rel + (k0 - q0)).astype(F32) * slope
        m1, l1, a1 = update(q1, kt, vt, bias, m1, l1, a1)
        m2, l2, a2 = update(q2, kt, vt, bias, m2, l2, a2)
        return m1, l1, a1, m2, l2, a2

    neg = jnp.full((tq, 1), -jnp.inf, F32)
    zl = jnp.zeros((tq, 1), F32)
    za = jnp.zeros((tq, ATT_V_DIM), F32)
    m1, l1, a1, m2, l2, a2 = lax.fori_loop(0, nk, body, (neg, zl, za, neg, zl, za))

    lv = lam_ref[...]
    lam = (jnp.exp(jnp.sum(lv[0:1] * lv[1:2], axis=-1, keepdims=True))
           - jnp.exp(jnp.sum(lv[2:3] * lv[3:4], axis=-1, keepdims=True)) + lambda_init)
    o = a1 / l1 - lam * (a2 / l2)
    o_ref[...] = _rms(o, sg_ref[...]) * (1.0 - lambda_init)


def _flash(qn, kt, vb, slopes, lamv, sg, nb, seq, lambda_init):
    nh = ATT_HEADS
    tq = _tile(seq, 512)
    tk = _tile(seq, 512)
    nq = seq // tq
    return pl.pallas_call(
        functools.partial(_flash_kernel, tk=tk, nk=seq // tk, lambda_init=lambda_init),
        grid=(nb, nh, nq),
        in_specs=[
            pl.BlockSpec((1, tq, LANES), lambda bb, h, i: (bb * nh + h, i, 0)),
            pl.BlockSpec((1, LANES, seq), lambda bb, h, i: (bb * nh + h, 0, 0)),
            pl.BlockSpec((1, seq, LANES), lambda bb, h, i: (bb * nh + h, 0, 0)),
            pl.BlockSpec((1, 1, LANES), lambda bb, h, i: (h, 0, 0)),
            _g3((4, ATT_QK_DIM)),
            _g3((1, ATT_V_DIM)),
        ],
        out_specs=pl.BlockSpec((tq, LANES), lambda bb, h, i: (bb * nq + i, h)),
        out_shape=jax.ShapeDtypeStruct((nb * seq, nh * ATT_V_DIM), F32),
        compiler_params=_cparams("parallel", "parallel", "arbitrary"),
        name="flash_diff_attn",
    )(qn, kt, vb, slopes, lamv, sg)


FLASH_TILE = 512
POS_U = 0
POS_W_LOW = 2
POS_OFF = 4
POS_SHIFT = 7
POS_W_UP = 8
BAND_EXP_CUTOFF = 112.0
SHIFT_SAFE_MAX = 40.0


def _split2(x):
    hi = x.astype(BF16).astype(F32)
    lo = (x - hi).astype(BF16).astype(F32)
    return hi, lo


def _split3(x):
    hi = x.astype(BF16).astype(F32)
    r = x - hi
    mid = r.astype(BF16).astype(F32)
    lo = (r - mid).astype(BF16).astype(F32)
    return hi, mid, lo


def _attprep_band_kernel(q_ref, k_ref, v_ref, qg_ref, kg_ref, slope_ref, qn_ref, kt_ref, vb_ref):
    tl = q_ref.shape[0]
    gr = lax.broadcasted_iota(jnp.int32, (3 * LANES, LANES), 0) % LANES
    gc = lax.broadcasted_iota(jnp.int32, (3 * LANES, LANES), 1)
    same_half = jnp.where((gr < ATT_QK_DIM) == (gc < ATT_QK_DIM), 1.0, 0.0).astype(BF16)

    def norm(x, g):
        pieces = jnp.concatenate(_split3(x * x), axis=1).astype(BF16)
        ms = jnp.dot(pieces, same_half, preferred_element_type=F32) * (1.0 / ATT_QK_DIM)
        return x * lax.rsqrt(ms + EPS) * g

    mq = jnp.max(jnp.abs(qg_ref[...]), axis=-1, keepdims=True)
    mk = jnp.max(jnp.abs(kg_ref[...]), axis=-1, keepdims=True)
    shift = mq * mk * (ATT_QK_DIM ** 0.5 * 1.02)

    r = lax.broadcasted_iota(jnp.int32, (LANES, tl), 0)
    j = lax.broadcasted_iota(jnp.int32, (LANES, tl), 1).astype(F32)
    minus_one = (r == POS_U) | (r == POS_U + 1) | ((r >= POS_OFF) & (r < POS_OFF + 3))
    pos_const = jnp.where(r == POS_SHIFT, -shift, jnp.where(minus_one, -1.0, 0.0))
    ones = jnp.ones((tl, LANES), BF16)

    for h in range(ATT_HEADS):
        hs = slice(h * LANES, (h + 1) * LANES)
        qn_ref[h] = (norm(q_ref[:, hs], qg_ref[...]) * (ATT_QK_DIM ** -0.5)).astype(BF16)
        kt_ref[h, 0:LANES, :] = norm(k_ref[:, hs], kg_ref[...]).T.astype(BF16)
        slope = slope_ref[h][:, 0:1]
        wl_hi, wl_lo = _split2(slope * (tl - 1.0 - j))
        wu_hi, wu_lo = _split2(slope * j)
        pos = jnp.where(r == POS_W_LOW, -wl_hi, pos_const)
        pos = jnp.where(r == POS_W_LOW + 1, -wl_lo, pos)
        pos = jnp.where(r == POS_W_UP, -wu_hi, pos)
        pos = jnp.where(r == POS_W_UP + 1, -wu_lo, pos)
        kt_ref[h, LANES:2 * LANES, :] = pos.astype(BF16)
        vb_ref[h, :, 0:LANES] = v_ref[:, hs].astype(BF16)
        vb_ref[h, :, LANES:2 * LANES] = ones


def _att_prep_band(proj, qg, kg, slopes, nb, seq):
    tl = _tile(seq, FLASH_TILE)
    nl = seq // tl
    nh = ATT_HEADS
    w = nh * LANES
    blk = lambda c: pl.BlockSpec((tl, w), lambda bb, i: (bb * nl + i, c))
    full = lambda shape: pl.BlockSpec(shape, lambda bb, i: (0,) * len(shape))
    return pl.pallas_call(
        _attprep_band_kernel,
        grid=(nb, nl),
        in_specs=[blk(0), blk(1), blk(2), full((1, LANES)), full((1, LANES)),
                  full((nh, 1, LANES))],
        out_specs=[
            pl.BlockSpec((nh, tl, LANES), lambda bb, i: (bb, i, 0)),
            pl.BlockSpec((nh, 2 * LANES, tl), lambda bb, i: (bb, 0, i)),
            pl.BlockSpec((nh, tl, 2 * LANES), lambda bb, i: (bb, i, 0)),
        ],
        out_shape=[
            jax.ShapeDtypeStruct((nb * nh, seq, LANES), BF16),
            jax.ShapeDtypeStruct((nb * nh, 2 * LANES, seq), BF16),
            jax.ShapeDtypeStruct((nb * nh, seq, 2 * LANES), BF16),
        ],
        compiler_params=_cparams("parallel", "parallel"),
        name="att_prep_band",
    )(proj, proj, proj, qg, kg, slopes)


def _flash_band_kernel(q_ref, kt_ref, v_ref, slope_ref, lam_ref, sg_ref, o_ref,
                       lhs_ref, acc_ref, p_ref, base_ref, bias_ref, *, tk, nk, lambda_init):
    tq = q_ref.shape[1]
    assert tq == tk
    h = pl.program_id(1)
    qi = pl.program_id(2)
    q0 = qi * tq
    slope_row = slope_ref[0]
    lane = lax.broadcasted_iota(jnp.int32, (tq, LANES), 1)
    rowf = lax.broadcasted_iota(jnp.int32, (tq, LANES), 0).astype(F32)

    q = q_ref[0]
    zero = jnp.zeros_like(q)
    first = lane < ATT_QK_DIM
    lhs_ref[0:tq] = jnp.where(first, q, zero)
    lhs_ref[tq:2 * tq] = jnp.where(first, zero, q)
    acc_ref[...] = jnp.zeros(acc_ref.shape, F32)

    def pos_base(u, w_lane):
        u_hi, u_lo = _split2(u)
        ones = (lane == POS_SHIFT) | (lane == w_lane) | (lane == w_lane + 1)
        base = jnp.where(ones, 1.0, 0.0)
        base = jnp.where(lane == POS_U, u_hi, base)
        return jnp.where(lane == POS_U + 1, u_lo, base)

    off_lanes = (lane >= POS_OFF) & (lane < POS_OFF + 3)
    lane1 = lax.broadcasted_iota(jnp.int32, (2 * SUBLANES, LANES), 1)
    slope_tile = jnp.broadcast_to(slope_row, (2 * SUBLANES, LANES))

    def pos_half(base, dist):
        if dist is not None:
            hi, mid, lo = _split3(slope_tile * dist.astype(F32))
            orow = jnp.where(lane1 == POS_OFF, hi, jnp.where(lane1 == POS_OFF + 1, mid, lo))
            base = jnp.where(off_lanes, orow[0:1, :], base)
        return base.astype(BF16)

    halves = (slice(0, tq), slice(tq, 2 * tq))

    def raw_scores(kidx, pos):
        k0 = pl.multiple_of(kidx * tk, tk)
        lhs = jnp.concatenate([jnp.concatenate([lhs_ref[rows], pos], axis=1) for rows in halves],
                              axis=0)
        return jnp.dot(lhs, kt_ref[0, :, pl.ds(k0, tk)], preferred_element_type=F32)

    def weighted_sum(kidx, slot):
        k0 = pl.multiple_of(kidx * tk, tk)
        acc_ref[...] += jnp.dot(p_ref[slot], v_ref[0, pl.ds(k0, tk), :], preferred_element_type=F32)


    band = lax.shift_left(jnp.int32(int(BAND_EXP_CUTOFF)), h + 1)
    k_lo = jnp.maximum(q0 + 1 - band, 0) // tk
    k_hi = jnp.minimum((q0 + tq - 2 + band + tk) // tk, nk)
    n_low = qi - k_lo
    n_off_diag = n_low + (k_hi - qi - 1)

    @pl.when(qi == 0)
    def _():
        base_ref[0] = pos_base(slope_row * rowf, POS_W_LOW)
        base_ref[1] = pos_base(slope_row * (tq - 1.0 - rowf), POS_W_UP)
        rel = (lax.broadcasted_iota(jnp.int32, (tq, tk), 1)
               - lax.broadcasted_iota(jnp.int32, (tq, tk), 0))
        bias_ref[...] = jnp.abs(rel).astype(F32) * slope_row[:, 0:1]

    s = raw_scores(qi, pos_half(jnp.where(lane == POS_SHIFT, 1.0, 0.0), None))
    for rows in halves:
        p_ref[0, rows] = jnp.exp(s[rows] - bias_ref[...]).astype(BF16)

    def tile_of(u):
        return jnp.where(u < n_low, k_lo + u, qi + 1 + u - n_low)

    def step(u, prev, s_in, s_out):
        below = u < n_low
        kidx = tile_of(u)
        dist = jnp.where(below, q0 - kidx * tk - (tk - 1), kidx * tk - q0 - (tq - 1))
        weighted_sum(prev, s_in)
        s = raw_scores(kidx, pos_half(base_ref[jnp.where(below, 0, 1)], dist))
        p_ref[s_out] = jnp.exp(s).astype(BF16)
        return kidx

    def run(u0, n_steps, prev):
        for j in range(n_steps):
            prev = step(u0 + j, prev, j, (j + 1) % n_steps)
        return prev

    done = 0
    prev = qi
    for width in (8, 4, 2):
        n_bodies = (n_off_diag - done) // width
        prev = lax.fori_loop(0, n_bodies,
                             lambda i, p, w=width, d=done: run(d + w * i, w, p), prev)
        done = done + width * n_bodies
    odd = n_off_diag - done

    @pl.when(odd == 1)
    def _():
        weighted_sum(step(n_off_diag - 1, prev, 0, 1), 1)

    @pl.when(odd == 0)
    def _():
        weighted_sum(prev, 0)

    lv = lam_ref[...]
    lam = (jnp.exp(jnp.sum(lv[0:1] * lv[1:2], axis=-1, keepdims=True))
           - jnp.exp(jnp.sum(lv[2:3] * lv[3:4], axis=-1, keepdims=True)) + lambda_init)
    a1 = acc_ref[0:tq, 0:LANES]
    l1 = acc_ref[0:tq, LANES:2 * LANES]
    a2 = acc_ref[tq:2 * tq, 0:LANES]
    l2 = acc_ref[tq:2 * tq, LANES:2 * LANES]
    o = a1 / l1 - lam * (a2 / l2)
    o_ref[...] = _rms(o, sg_ref[...]) * (1.0 - lambda_init)


def _flash_band(qn, kt, vb, slopes, lamv, sg, nb, seq, lambda_init):
    nh = ATT_HEADS
    tq = _tile(seq, FLASH_TILE)
    nq = seq // tq
    return pl.pallas_call(
        functools.partial(_flash_band_kernel, tk=tq, nk=nq, lambda_init=lambda_init),
        grid=(nb, nh, nq),
        in_specs=[
            pl.BlockSpec((1, tq, LANES), lambda bb, h, i: (bb * nh + h, i, 0)),
            pl.BlockSpec((1, 2 * LANES, seq), lambda bb, h, i: (bb * nh + h, 0, 0)),
            pl.BlockSpec((1, seq, 2 * LANES), lambda bb, h, i: (bb * nh + h, 0, 0)),
            pl.BlockSpec((1, 1, LANES), lambda bb, h, i: (h, 0, 0)),
            _g3((4, ATT_QK_DIM)),
            _g3((1, ATT_V_DIM)),
        ],
        out_specs=pl.BlockSpec((tq, LANES), lambda bb, h, i: (bb * nq + i, h)),
        out_shape=jax.ShapeDtypeStruct((nb * seq, nh * ATT_V_DIM), F32),
        scratch_shapes=[
            pltpu.VMEM((2 * tq, LANES), BF16),
            pltpu.VMEM((2 * tq, 2 * LANES), F32),
            pltpu.VMEM((8, 2 * tq, tq), BF16),
            pltpu.VMEM((2, tq, LANES), F32),
            pltpu.VMEM((tq, tq), F32),
        ],
        compiler_params=_cparams("parallel", "parallel", "arbitrary"),
        name="flash_band",
    )(qn, kt, vb, slopes, lamv, sg)


def _attention(proj, lw, slopes, nb, seq, lambda_init):
    def banded(proj):
        qn, kt, vb = _att_prep_band(proj, lw["qg"], lw["kg"], slopes, nb, seq)
        return _flash_band(qn, kt, vb, slopes, lw["lam"], lw["sub_g"], nb, seq, lambda_init)

    def online(proj):
        qn, kt, vb = _att_prep(proj, lw["qg"], lw["kg"], nb, seq)
        return _flash(qn, kt, vb, slopes, lw["lam"], lw["sub_g"], nb, seq, lambda_init)

    shift = jnp.max(jnp.abs(lw["qg"])) * jnp.max(jnp.abs(lw["kg"])) * (ATT_QK_DIM ** 0.5 * 1.02)
    return lax.cond(shift < SHIFT_SAFE_MAX, banded, online, proj)


def _alibi_slopes():
    s = np.array([2.0 ** (-8.0 * (i + 1) / ATT_HEADS) for i in range(ATT_HEADS)], dtype=np.float32)
    return jnp.asarray(np.broadcast_to(s[:, None, None], (ATT_HEADS, 1, LANES)).copy())


def _prep_weights(w):
    row = lambda v: v.reshape(1, -1).astype(F32)
    pad_lanes = lambda v: jnp.pad(v.reshape(1, -1).astype(F32), ((0, 0), (0, LANES - v.size)))
    out = {"layers": [], "slopes": _alibi_slopes()}
    for i in range(DEPTH):
        j = i // 2
        lw = {
            "pre_g": row(w["pre_norm_g"][i]),
            "wg": w["ple_w_gate"][i].astype(BF16),
            "wp": w["ple_w_proj"][i].astype(BF16),
            "gng": row(w["ple_gate_norm_g"][i]),
            "png": row(w["ple_norm_g"][i]),
        }
        if i % 2 == 0:
            win = w["ssd_w_in"][j]
            lw.update(
                w_main=win[:, :SSD_MAIN_COLS].astype(BF16),
                w_dt=jnp.pad(win[:, SSD_MAIN_COLS:], ((0, 0), (0, LANES - 2 * SSD_HEADS))).astype(BF16),
                dt_bias=pad_lanes(w["ssd_dt_bias"][j]),
                a_log=pad_lanes(w["ssd_a_log"][j]),
                conv_w=w["ssd_conv_w"][j].astype(F32),
                conv_b=row(w["ssd_conv_b"][j]),
                d_skip=row(jnp.repeat(w["ssd_d_skip"][j], SSD_HEAD_DIM)),
                norm_g=row(w["ssd_norm_g"][j]),
                w_out=w["ssd_w_out"][j].astype(BF16),
            )
        else:
            lw.update(
                w_in=w["att_w_in"][j].astype(BF16),
                qg=row(jnp.tile(w["att_q_norm_g"][j], 2)),
                kg=row(jnp.tile(w["att_k_norm_g"][j], 2)),
                lam=jnp.stack([w["att_lam_q1"][j], w["att_lam_k1"][j],
                               w["att_lam_q2"][j], w["att_lam_k2"][j]]).astype(F32),
                sub_g=row(w["att_sub_norm_g"][j]),
                w_out=w["att_w_out"][j].astype(BF16),
            )
        out["layers"].append(lw)
    return out


def _trunk(x3, p4, pw):
    nb, seq, d = x3.shape
    t = nb * seq
    x = x3.reshape(t, d)
    for i in range(DEPTH):
        lw = pw["layers"][i]
        p = p4[i].reshape(t, PLE_DIM)
        if i % 2 == 0:
            proj, dt = _inproj(x, lw["pre_g"], lw["w_main"], lw["w_dt"], lw["dt_bias"])
            xs = _conv_silu(proj, lw["conv_w"], lw["conv_b"], nb, seq,
                            SSD_INNER, 0, SSD_INNER, F32)
            bc = _conv_silu(proj, lw["conv_w"], lw["conv_b"], nb, seq,
                            2 * SSD_INNER, SSD_INNER, 2 * SSD_GN, BF16)
            yf = _ssd_scan(xs, bc, dt, lw["a_log"], nb, seq)
            y = _ssd_scan(xs, bc, dt, lw["a_log"], nb, seq, yf, lw["d_skip"])
            x = _ssd_post(y, proj, x, p, lw["norm_g"], lw["w_out"],
                          lw["wg"], lw["wp"], lw["gng"], lw["png"])
        else:
            lambda_init = 0.8 - 0.6 * math.exp(-0.3 * i)
            proj = _inproj(x, lw["pre_g"], lw["w_in"])
            o = _attention(proj, lw, pw["slopes"], nb, seq, lambda_init)
            x = _att_post(o, proj, x, p, lw["w_out"], lw["wg"], lw["wp"], lw["gng"], lw["png"])
    return x.reshape(nb, seq, d)


def kernel(x_prompt, x_sample, p_prompt, p_sample, pre_norm_g, ssd_w_in, ssd_conv_w, ssd_conv_b, ssd_dt_bias, ssd_a_log, ssd_d_skip, ssd_norm_g, ssd_w_out, att_w_in, att_q_norm_g, att_k_norm_g, att_lam_q1, att_lam_k1, att_lam_q2, att_lam_k2, att_sub_norm_g, att_w_out, ple_w_proj, ple_norm_g, ple_gate_norm_g, ple_w_gate):
    pw = _prep_weights(dict(
        pre_norm_g=pre_norm_g, ssd_w_in=ssd_w_in, ssd_conv_w=ssd_conv_w, ssd_conv_b=ssd_conv_b,
        ssd_dt_bias=ssd_dt_bias, ssd_a_log=ssd_a_log, ssd_d_skip=ssd_d_skip, ssd_norm_g=ssd_norm_g,
        ssd_w_out=ssd_w_out, att_w_in=att_w_in, att_q_norm_g=att_q_norm_g, att_k_norm_g=att_k_norm_g,
        att_lam_q1=att_lam_q1, att_lam_k1=att_lam_k1, att_lam_q2=att_lam_q2, att_lam_k2=att_lam_k2,
        att_sub_norm_g=att_sub_norm_g, att_w_out=att_w_out, ple_w_proj=ple_w_proj,
        ple_norm_g=ple_norm_g, ple_gate_norm_g=ple_gate_norm_g, ple_w_gate=ple_w_gate))
    return (_trunk(x_prompt, p_prompt, pw), _trunk(x_sample, p_sample, pw))
```
